```python
import jax, jax.numpy as jnp
from jax import lax
import numpy as np

D_MODEL = 1024
BATCH = 8
SEQ = 4096
DEPTH = 2

D_MIX = D_MODEL
D_FOURIER = D_MIX // 2
F_GROUPS = 4
F_GROUP_DIM = D_FOURIER // F_GROUPS
D_GLA_V = D_MIX - D_FOURIER
GLA_HEADS = 4
DV_HEAD = D_GLA_V // GLA_HEADS
DK_HEAD = DV_HEAD // 2
D_GLA_K = GLA_HEADS * DK_HEAD
GATE_RANK = 16
GATE_LOGIT_NORMALIZER = 16.0
CHUNK = 64
D_FF = -(-8 * D_MODEL // (3 * 256)) * 256
EPS = 1e-6
IN_SIZES = (D_FOURIER, D_GLA_K, D_GLA_K, D_GLA_V, D_GLA_V, GATE_RANK, GATE_RANK)
IN_COLS = sum(IN_SIZES)
IN_OFFSETS = tuple(int(o) for o in np.cumsum(IN_SIZES)[:-1])

kernel_name = "hybrid_fnet_gla_sandwich_encoder"


def rmsnorm(x, w):
    xf = x.astype(jnp.float32)
    y = xf * lax.rsqrt(jnp.mean(xf * xf, axis=-1, keepdims=True) + EPS)
    return (y * w.astype(jnp.float32)).astype(x.dtype)


def fourier_mix(u):
    b, s, _ = u.shape
    uf = u.astype(jnp.float32).reshape(b, s, F_GROUPS, F_GROUP_DIM)
    y = jnp.fft.fftn(uf, axes=(1, 3), norm="ortho").real
    return y.reshape(b, s, D_FOURIER).astype(u.dtype)


def gla_chunked(q, k, v, log_a):
    b, h, s, dk = q.shape
    dv = v.shape[-1]
    n = s // CHUNK
    q = q.reshape(b, h, n, CHUNK, dk)
    k = k.reshape(b, h, n, CHUNK, dk)
    v = v.reshape(b, h, n, CHUNK, dv)
    cum = jnp.cumsum(log_a.reshape(b, h, n, CHUNK, dk), axis=3)
    cum_last = cum[:, :, :, -1:, :]
    q_e = q * jnp.exp(cum)
    k_e = k * jnp.exp(-cum)
    k_end = k * jnp.exp(cum_last - cum)
    mask = jnp.tril(jnp.ones((CHUNK, CHUNK), dtype=bool))
    att = jnp.where(mask, jnp.einsum('bhnid,bhnjd->bhnij', q_e, k_e), 0.0)
    o_intra = jnp.einsum('bhnij,bhnjv->bhniv', att, v)
    chunk_state = jnp.einsum('bhnjd,bhnjv->bhndv', k_end, v)
    decay = jnp.exp(cum_last[:, :, :, 0, :])

    def step(state, inp):
        dec, cs = inp
        return state * dec[..., None] + cs, state

    init = jnp.zeros((b, h, dk, dv), q.dtype)
    _, prev = lax.scan(step, init, (jnp.moveaxis(decay, 2, 0), jnp.moveaxis(chunk_state, 2, 0)))
    prev = jnp.moveaxis(prev, 0, 2)
    o_inter = jnp.einsum('bhnid,bhndv->bhniv', q_e, prev)
    return (o_intra + o_inter).reshape(b, h, s, dv)


def gla_bidirectional(h_in, qp, kp, vp, gp, af, ab, w_af, b_af, w_ab, b_ab, w_onorm):
    b, s, _ = qp.shape
    dt = qp.dtype
    heads = lambda t, d: t.astype(jnp.float32).reshape(b, s, GLA_HEADS, d).transpose(0, 2, 1, 3)
    q = heads(qp, DK_HEAD) * (DK_HEAD ** -0.5)
    k = heads(kp, DK_HEAD)
    v = heads(vp, DV_HEAD)
    la_f = jax.nn.log_sigmoid((af @ w_af + b_af).astype(jnp.float32)) / GATE_LOGIT_NORMALIZER
    la_b = jax.nn.log_sigmoid((ab @ w_ab + b_ab).astype(jnp.float32)) / GATE_LOGIT_NORMALIZER
    la_f = heads(la_f, DK_HEAD)
    la_b = heads(la_b, DK_HEAD)
    o_fwd = gla_chunked(q, k, v, la_f)
    flip = lambda t: jnp.flip(t, axis=2)
    o_bwd = flip(gla_chunked(flip(q), flip(k), flip(v), flip(la_b)))
    o = (o_fwd + o_bwd).transpose(0, 2, 1, 3)
    o = rmsnorm(o, w_onorm)
    g = jax.nn.silu(gp.astype(jnp.float32)).reshape(b, s, GLA_HEADS, DV_HEAD)
    return (o * g).reshape(b, s, D_GLA_V).astype(dt)


def setup_inputs(seed: int = 0) -> dict:
    key = jax.random.key(seed)
    ks = jax.random.split(key, 16)
    nrm = lambda k, shape, scale: jax.random.normal(k, shape, jnp.float32) * scale
    gain = lambda k: 1.0 + 0.02 * jax.random.normal(k, (DEPTH, D_MODEL), jnp.float32)
    return {
        "x": jax.random.normal(ks[0], (BATCH, SEQ, D_MODEL), jnp.float32),
        "norm_mix_pre": gain(ks[1]),
        "w_in": nrm(ks[2], (DEPTH, D_MODEL, IN_COLS), D_MODEL ** -0.5),
        "w_alpha_fwd": nrm(ks[3], (DEPTH, GATE_RANK, D_GLA_K), GATE_RANK ** -0.5),
        "b_alpha_fwd": nrm(ks[4], (DEPTH, D_GLA_K), 0.1),
        "w_alpha_bwd": nrm(ks[5], (DEPTH, GATE_RANK, D_GLA_K), GATE_RANK ** -0.5),
        "b_alpha_bwd": nrm(ks[6], (DEPTH, D_GLA_K), 0.1),
        "gla_out_norm": 1.0 + 0.02 * jax.random.normal(ks[7], (DEPTH, DV_HEAD), jnp.float32),
        "w_out": nrm(ks[8], (DEPTH, D_MIX, D_MODEL), D_MIX ** -0.5),
        "norm_mix_post": gain(ks[9]),
        "norm_ffn_pre": gain(ks[10]),
        "w_ffn_gate": nrm(ks[11], (DEPTH, D_MODEL, D_FF), D_MODEL ** -0.5),
        "w_ffn_up": nrm(ks[12], (DEPTH, D_MODEL, D_FF), D_MODEL ** -0.5),
        "w_ffn_down": nrm(ks[13], (DEPTH, D_FF, D_MODEL), D_FF ** -0.5),
        "norm_ffn_post": gain(ks[14]),
    }


def reference(x, norm_mix_pre, w_in, w_alpha_fwd, b_alpha_fwd, w_alpha_bwd, b_alpha_bwd,
              gla_out_norm, w_out, norm_mix_post, norm_ffn_pre, w_ffn_gate, w_ffn_up,
              w_ffn_down, norm_ffn_post):
    for l in range(DEPTH):
        h = rmsnorm(x, norm_mix_pre[l])
        p = h @ w_in[l]
        fp, qp, kp, vp, gp, af, ab = jnp.split(p, IN_OFFSETS, axis=-1)
        y_f = fourier_mix(fp)
        y_g = gla_bidirectional(h, qp, kp, vp, gp, af, ab, w_alpha_fwd[l], b_alpha_fwd[l],
                                w_alpha_bwd[l], b_alpha_bwd[l], gla_out_norm[l])
        m = jnp.concatenate([y_f, y_g], axis=-1) @ w_out[l]
        x = x + rmsnorm(m, norm_mix_post[l])
        h2 = rmsnorm(x, norm_ffn_pre[l])
        f = (jax.nn.silu(h2 @ w_ffn_gate[l]) * (h2 @ w_ffn_up[l])) @ w_ffn_down[l]
        x = x + rmsnorm(f, norm_ffn_post[l])
    return x
```

```python
import functools

import numpy as np
import jax
import jax.numpy as jnp
from jax import lax
from jax.experimental import pallas as pl
from jax.experimental.pallas import tpu as pltpu

F_GROUPS = 4
GLA_HEADS = 4
GATE_RANK = 16
GATE_LOGIT_NORMALIZER = 16.0
CHUNK = 64
EPS = 1e-6

LANES = 128
VMEM_LIMIT = 56 * 1024 * 1024

BF16 = jnp.bfloat16
F32 = jnp.float32


def _rms(x, w):
    return x * lax.rsqrt(jnp.mean(x * x, axis=-1, keepdims=True) + EPS) * w


def _dot(a, b):
    return jnp.dot(a, b, preferred_element_type=F32)


def _const_spec(shape):
    return pl.BlockSpec(shape, lambda *_: (0,) * len(shape))


def _dft_constants(seq, group_dim):
    n1 = int(round(seq ** 0.5))
    assert n1 * n1 == seq
    c = np.arange(group_dim)
    ang = 2.0 * np.pi * ((c[:, None] * c[None, :]) % group_dim) / group_dim
    fc = np.concatenate([np.cos(ang), -np.sin(ang)], axis=1) / np.sqrt(group_dim)
    a = np.arange(n1)
    ang1 = 2.0 * np.pi * ((a[:, None] * a[None, :]) % n1) / n1
    c1, s1 = np.cos(ang1), np.sin(ang1)
    m1 = np.block([[c1, s1], [-s1, c1]]) / n1
    k1 = np.arange(n1)[:, None, None]
    k2 = np.arange(n1)[None, :, None]
    b = np.arange(n1)[None, None, :]
    ang2 = 2.0 * np.pi * ((b * (k1 + n1 * k2)) % seq) / seq
    g = np.concatenate([np.cos(ang2), np.sin(ang2)], axis=2)
    return (jnp.asarray(fc, F32).astype(BF16), jnp.asarray(m1, F32).astype(BF16),
            jnp.asarray(g, F32).astype(BF16))


def _in_proj_kernel(x_ref, nw_ref, wf_ref, wq_ref, wk_ref, wv_ref, wg_ref, wab_ref, fc_ref,
                    wgate_ref, bgate_ref,
                    zr_ref, zi_ref, q_ref, k_ref, v_ref, g_ref, laf_ref, lab_ref, *, group_dim):
    h = _rms(x_ref[...], nw_ref[...]).astype(BF16)
    fp = _dot(h, wf_ref[...]).astype(BF16)
    d_f = fp.shape[1]
    for grp in range(d_f // group_dim):
        sl = slice(grp * group_dim, (grp + 1) * group_dim)
        zz = _dot(fp[:, sl], fc_ref[...])
        zr_ref[:, sl] = zz[:, :group_dim].astype(zr_ref.dtype)
        zi_ref[:, sl] = zz[:, group_dim:].astype(zi_ref.dtype)
    q_ref[...] = _dot(h, wq_ref[...])
    k_ref[...] = _dot(h, wk_ref[...])
    v_ref[...] = _dot(h, wv_ref[...])
    g_ref[...] = _dot(h, wg_ref[...])
    ab = _dot(h, wab_ref[...]).astype(BF16)
    logits = _dot(ab, wgate_ref[...]) + bgate_ref[...]
    la = (jnp.minimum(logits, 0.0) - jnp.log1p(jnp.exp(-jnp.abs(logits)))) * (1.0 / GATE_LOGIT_NORMALIZER)
    d_k = laf_ref.shape[1]
    laf_ref[...] = la[:, :d_k]
    lab_ref[...] = la[:, d_k:]


def _in_proj(x2, nw, w_in, w_af, b_af, w_ab, b_ab, fc, *, tm):
    t, d = x2.shape
    d_f = d // 2
    d_v = d - d_f
    d_k = d_v // 2
    group_dim = d_f // F_GROUPS
    o = np.cumsum([0, d_f, d_k, d_k, d_v, d_v, GATE_RANK, GATE_RANK])
    wb = w_in.astype(BF16)
    wf, wq, wk, wv, wg = (wb[:, o[i]:o[i + 1]] for i in range(5))
    wab = jnp.pad(wb[:, o[5]:o[7]], ((0, 0), (0, LANES - 2 * GATE_RANK)))
    wgate = jnp.zeros((LANES, 2 * d_k), F32)
    wgate = wgate.at[:GATE_RANK, :d_k].set(w_af).at[GATE_RANK:2 * GATE_RANK, d_k:].set(w_ab).astype(BF16)
    bgate = jnp.concatenate([b_af, b_ab])[None, :]
    row = lambda n: pl.BlockSpec((tm, n), lambda i: (i, 0))
    outs = [(d_f, BF16), (d_f, BF16), (d_k, F32), (d_k, F32), (d_v, F32), (d_v, F32), (d_k, F32), (d_k, F32)]
    return pl.pallas_call(
        functools.partial(_in_proj_kernel, group_dim=group_dim),
        grid=(t // tm,),
        in_specs=[row(d), _const_spec((1, d)), _const_spec(wf.shape), _const_spec(wq.shape),
                  _const_spec(wk.shape), _const_spec(wv.shape), _const_spec(wg.shape),
                  _const_spec(wab.shape), _const_spec(fc.shape), _const_spec(wgate.shape),
                  _const_spec(bgate.shape)],
        out_specs=[row(n) for n, _ in outs],
        out_shape=[jax.ShapeDtypeStruct((t, n), dt) for n, dt in outs],
        compiler_params=pltpu.CompilerParams(dimension_semantics=("parallel",),
                                             vmem_limit_bytes=VMEM_LIMIT),
        name="in_proj",
    )(x2, nw[None, :], wf, wq, wk, wv, wg, wab, fc, wgate, bgate)


def _fft1_kernel(zr_ref, zi_ref, m1_ref, or_ref, oi_ref):
    n1 = zr_ref.shape[0]
    rhs = jnp.concatenate([zr_ref[...], zi_ref[...]], axis=0)
    out = _dot(m1_ref[...], rhs)
    or_ref[...] = out[:n1].astype(or_ref.dtype)
    oi_ref[...] = out[n1:].astype(oi_ref.dtype)


def _fft1(zr, zi, m1, *, batch, seq, tl):
    n1 = m1.shape[0] // 2
    cols = zr.shape[1]
    wide = (seq // n1) * cols
    view = lambda z: z.reshape(batch * n1, wide)
    spec = pl.BlockSpec((n1, tl), lambda b, l: (b, l))
    o_r, o_i = pl.pallas_call(
        _fft1_kernel,
        grid=(batch, wide // tl),
        in_specs=[spec, spec, _const_spec(m1.shape)],
        out_specs=[spec, spec],
        out_shape=[jax.ShapeDtypeStruct((batch * n1, wide), BF16)] * 2,
        compiler_params=pltpu.CompilerParams(dimension_semantics=("parallel", "parallel"),
                                             vmem_limit_bytes=VMEM_LIMIT),
        name="fft1",
    )(view(zr), view(zi), m1)
    return o_r.reshape(batch, seq, cols), o_i.reshape(batch, seq, cols)


def _fft2_kernel(zr_ref, zi_ref, g_ref, o_ref):
    tk1, n2, _ = g_ref.shape
    cols = zr_ref.shape[2]
    for j in range(tk1):
        rows = slice(j * n2, (j + 1) * n2)
        rhs = jnp.concatenate([zr_ref[0, rows, :], zi_ref[0, rows, :]], axis=0)
        o_ref[0, :, j * cols:(j + 1) * cols] = _dot(g_ref[j], rhs).astype(o_ref.dtype)


def _fft2(z1r, z1i, g, *, tk1):
    batch, seq, cols = z1r.shape
    n1, n2, _ = g.shape
    in_spec = pl.BlockSpec((1, tk1 * n2, cols), lambda b, i: (b, i, 0))
    out = pl.pallas_call(
        _fft2_kernel,
        grid=(batch, n1 // tk1),
        in_specs=[in_spec, in_spec, pl.BlockSpec((tk1, n2, 2 * n2), lambda b, i: (i, 0, 0))],
        out_specs=pl.BlockSpec((1, n2, tk1 * cols), lambda b, i: (b, 0, i)),
        out_shape=jax.ShapeDtypeStruct((batch, n2, n1 * cols), BF16),
        compiler_params=pltpu.CompilerParams(dimension_semantics=("parallel", "parallel"),
                                             vmem_limit_bytes=VMEM_LIMIT),
        name="fft2",
    )(z1r, z1i, g)
    return out.reshape(batch * seq, cols)


def _gla_chunk(q, k, v, la, s_ref, *, reverse):
    c, dk2 = q.shape
    dv2 = v.shape[1]
    dk, dv = dk2 // 2, dv2 // 2
    ri = lax.broadcasted_iota(jnp.int32, (c, c), 0)
    ci = lax.broadcasted_iota(jnp.int32, (c, c), 1)
    tri = jnp.where((ci >= ri) if reverse else (ci <= ri), 1.0, 0.0).astype(BF16)
    la_hi = la.astype(BF16)
    la_lo = (la - la_hi.astype(F32)).astype(BF16)
    cum = _dot(tri, la_hi) + _dot(tri, la_lo)
    tot = cum[0:1, :] if reverse else cum[c - 1:c, :]
    qe = (q * (jnp.exp(cum) * (dk ** -0.5))).astype(BF16)
    ke = k * jnp.exp(-cum)
    kend = (k * jnp.exp(tot - cum)).astype(BF16)

    lane_k = lax.broadcasted_iota(jnp.int32, (c, dk2), 1)
    ke_bd = jnp.concatenate([jnp.where(lane_k < dk, ke, 0.0), jnp.where(lane_k >= dk, ke, 0.0)],
                            axis=0).astype(BF16)
    att = lax.dot_general(qe, ke_bd, (((1,), (1,)), ((), ())), preferred_element_type=F32)
    ri2 = lax.broadcasted_iota(jnp.int32, (c, 2 * c), 0)
    cj2 = lax.broadcasted_iota(jnp.int32, (c, 2 * c), 1) % c
    keep = (cj2 >= ri2) if reverse else (cj2 <= ri2)
    att = jnp.where(keep, att, 0.0).astype(BF16)

    lane_v = lax.broadcasted_iota(jnp.int32, (c, dv2), 1)
    vb = v.astype(BF16)
    v_bd = jnp.concatenate([jnp.where(lane_v < dv, v, 0.0), jnp.where(lane_v >= dv, v, 0.0)],
                           axis=0).astype(BF16)
    s_prev = s_ref[...]
    o = _dot(att, v_bd) + _dot(qe, s_prev.astype(BF16))

    cs = lax.dot_general(kend, vb, (((0,), (0,)), ((), ())), preferred_element_type=F32)
    dec = jnp.exp(tot)
    dec_hi = dec.astype(BF16).astype(F32)
    dec_lo = dec - dec_hi
    row_k = lax.broadcasted_iota(jnp.int32, (c, dk2), 0)
    dsel = jnp.where(row_k == 0, dec_hi, jnp.where(row_k == 1, dec_lo, 0.0)).astype(BF16)
    ones_rows = jnp.where(lax.broadcasted_iota(jnp.int32, (c, dv2), 0) < 2, 1.0, 0.0).astype(BF16)
    dec_col = lax.dot_general(dsel, ones_rows, (((0,), (0,)), ((), ())),
                              preferred_element_type=F32)
    rk = lax.broadcasted_iota(jnp.int32, (dk2, dv2), 0) // dk
    cv = lax.broadcasted_iota(jnp.int32, (dk2, dv2), 1) // dv
    s_ref[...] = s_prev * dec_col + jnp.where(rk == cv, cs, 0.0)
    return o


def _gla_kernel(qf_ref, kf_ref, vf_ref, lf_ref, qb_ref, kb_ref, vb_ref, lb_ref,
                of_ref, ob_ref, sf_ref, sb_ref):
    @pl.when(pl.program_id(2) == 0)
    def _():
        sf_ref[...] = jnp.zeros_like(sf_ref)
        sb_ref[...] = jnp.zeros_like(sb_ref)

    nc = qf_ref.shape[1] // CHUNK
    for c in range(nc):
        rf = slice(c * CHUNK, (c + 1) * CHUNK)
        of_ref[0, rf, :] = _gla_chunk(qf_ref[0, rf, :], kf_ref[0, rf, :], vf_ref[0, rf, :],
                                      lf_ref[0, rf, :], sf_ref, reverse=False)
        rb = slice((nc - 1 - c) * CHUNK, (nc - c) * CHUNK)
        ob_ref[0, rb, :] = _gla_chunk(qb_ref[0, rb, :], kb_ref[0, rb, :], vb_ref[0, rb, :],
                                      lb_ref[0, rb, :], sb_ref, reverse=True)


def _gla(q, k, v, laf, lab, *, rows):
    b, s, dk_all = q.shape
    dv_all = v.shape[2]
    pairs = GLA_HEADS // 2
    dk2, dv2 = dk_all // pairs, dv_all // pairs
    nblk = s // rows
    fwd = lambda w: pl.BlockSpec((1, rows, w), lambda bi, p, j: (bi, j, p))
    bwd = lambda w: pl.BlockSpec((1, rows, w), lambda bi, p, j: (bi, nblk - 1 - j, p))
    return pl.pallas_call(
        _gla_kernel,
        grid=(b, pairs, nblk),
        in_specs=[fwd(dk2), fwd(dk2), fwd(dv2), fwd(dk2), bwd(dk2), bwd(dk2), bwd(dv2), bwd(dk2)],
        out_specs=[fwd(dv2), bwd(dv2)],
        out_shape=[jax.ShapeDtypeStruct((b, s, dv_all), F32)] * 2,
        scratch_shapes=[pltpu.VMEM((dk2, dv2), F32), pltpu.VMEM((dk2, dv2), F32)],
        compiler_params=pltpu.CompilerParams(dimension_semantics=("parallel", "parallel", "arbitrary"),
                                             vmem_limit_bytes=VMEM_LIMIT),
        name="gla",
    )(q, k, v, laf, q, k, v, lab)


def _post_kernel(x_ref, yf_ref, of_ref, ob_ref, g_ref, onw_ref, wo_f_ref, wo_g_ref, npost_ref,
                 nffn_ref, wgt_ref, wup_ref, wdn_ref, nffn_post_ref, o_ref, *, dv, ff_chunk):
    x = x_ref[...]
    o = of_ref[...] + ob_ref[...]
    g = g_ref[...]
    onw = onw_ref[...]
    yg = []
    for hd in range(o.shape[1] // dv):
        sl = slice(hd * dv, (hd + 1) * dv)
        gh = g[:, sl]
        yg.append((_rms(o[:, sl], onw) * (gh * jax.nn.sigmoid(gh))).astype(BF16))
    yg = jnp.concatenate(yg, axis=1)
    m = _dot(yf_ref[...], wo_f_ref[...]) + _dot(yg, wo_g_ref[...])
    x = x + _rms(m, npost_ref[...])

    h2 = _rms(x, nffn_ref[...]).astype(BF16)
    d_ff = wgt_ref.shape[1]
    f = None
    for c0 in range(0, d_ff, ff_chunk):
        sl = slice(c0, c0 + ff_chunk)
        gt = _dot(h2, wgt_ref[:, sl])
        up = _dot(h2, wup_ref[:, sl])
        act = (gt * jax.nn.sigmoid(gt) * up).astype(BF16)
        part = _dot(act, wdn_ref[sl, :])
        f = part if f is None else f + part
    o_ref[...] = x + _rms(f, nffn_post_ref[...])


def _post(x2, yf, of, ob, g, onw, w_out, npost, nffn, w_gate, w_up, w_down, nffn_post, *, tm, ff_chunk):
    t, d = x2.shape
    d_f = yf.shape[1]
    d_v = of.shape[1]
    dv = d_v // GLA_HEADS
    wo = w_out.astype(BF16)
    wo_f, wo_g = wo[:d_f], wo[d_f:]
    wgt, wup, wdn = w_gate.astype(BF16), w_up.astype(BF16), w_down.astype(BF16)
    row = lambda n: pl.BlockSpec((tm, n), lambda i: (i, 0))
    single = lambda a: pl.BlockSpec(a.shape, lambda i: (0,) * a.ndim, pipeline_mode=pl.Buffered(1))
    vec = lambda a: a[None, :]
    consts = [vec(onw), wo_f, wo_g, vec(npost), vec(nffn), wgt, wup, wdn, vec(nffn_post)]
    return pl.pallas_call(
        functools.partial(_post_kernel, dv=dv, ff_chunk=ff_chunk),
        grid=(t // tm,),
        in_specs=[row(d), row(d_f), row(d_v), row(d_v), row(d_v)] + [single(a) for a in consts],
        out_specs=row(d),
        out_shape=jax.ShapeDtypeStruct((t, d), F32),
        compiler_params=pltpu.CompilerParams(dimension_semantics=("parallel",),
                                             vmem_limit_bytes=VMEM_LIMIT),
        name="post",
    )(x2, yf, of, ob, g, *consts)


def kernel(x, norm_mix_pre, w_in, w_alpha_fwd, b_alpha_fwd, w_alpha_bwd, b_alpha_bwd, gla_out_norm,
           w_out, norm_mix_post, norm_ffn_pre, w_ffn_gate, w_ffn_up, w_ffn_down, norm_ffn_post):
    batch, seq, d = x.shape
    depth = w_in.shape[0]
    d_f = d // 2
    fc, m1, gtab = _dft_constants(seq, d_f // F_GROUPS)
    x2 = x.reshape(batch * seq, d)
    for l in range(depth):
        zr, zi, q, k, v, g, laf, lab = _in_proj(
            x2, norm_mix_pre[l], w_in[l], w_alpha_fwd[l], b_alpha_fwd[l], w_alpha_bwd[l], b_alpha_bwd[l],
            fc, tm=512)
        z1r, z1i = _fft1(zr, zi, m1, batch=batch, seq=seq, tl=8192)
        yf = _fft2(z1r, z1i, gtab, tk1=8)
        r3 = lambda a: a.reshape(batch, seq, a.shape[1])
        of, ob = _gla(r3(q), r3(k), r3(v), r3(laf), r3(lab), rows=512)
        flat = lambda a: a.reshape(batch * seq, a.shape[2])
        x2 = _post(x2, yf, flat(of), flat(ob), g, gla_out_norm[l], w_out[l], norm_mix_post[l],
                   norm_ffn_pre[l], w_ffn_gate[l], w_ffn_up[l], w_ffn_down[l], norm_ffn_post[l],
                   tm=512, ff_chunk=256)
    return x2.reshape(batch, seq, d)
```

```python
import functools

import numpy as np
import jax
import jax.numpy as jnp
from jax import lax
from jax.experimental import pallas as pl
from jax.experimental.pallas import tpu as pltpu

F_GROUPS = 4
GLA_HEADS = 4
GATE_RANK = 16
GATE_LOGIT_NORMALIZER = 16.0
CHUNK = 64
EPS = 1e-6

LANES = 128
SUBLANES = 8
TB = SUBLANES
VMEM_LIMIT = 56 * 1024 * 1024

BF16 = jnp.bfloat16
F32 = jnp.float32


def _rms(x, w):
    return x * lax.rsqrt(jnp.mean(x * x, axis=-1, keepdims=True) + EPS) * w


def _dot(a, b):
    return jnp.dot(a, b, preferred_element_type=F32)


def _const_spec(a):
    return pl.BlockSpec(a.shape, lambda *_: (0,) * a.ndim, pipeline_mode=pl.Buffered(1))


def _tile_spec(n1, width):
    nb = n1 // TB
    return pl.BlockSpec((None, n1, TB, width), lambda i: (i // nb, 0, i % nb, 0))


def _dft_constants(seq, group_dim):
    n1 = int(round(seq ** 0.5))
    assert n1 * n1 == seq
    c = np.arange(group_dim)
    ang = 2.0 * np.pi * ((c[:, None] * c[None, :]) % group_dim) / group_dim
    fc = np.concatenate([np.cos(ang), -np.sin(ang)], axis=1) / np.sqrt(group_dim)
    a = np.arange(n1)
    ang1 = 2.0 * np.pi * ((a[:, None] * a[None, :]) % n1) / n1
    c1, s1 = np.cos(ang1), np.sin(ang1)
    m1 = np.block([[c1, s1], [-s1, c1]]) / n1
    k1 = np.arange(n1)[:, None, None]
    k2 = np.arange(n1)[None, :, None]
    b = np.arange(n1)[None, None, :]
    ang2 = 2.0 * np.pi * ((b * (k1 + n1 * k2)) % seq) / seq
    g = np.concatenate([np.cos(ang2), np.sin(ang2)], axis=2)
    return (jnp.asarray(fc, F32).astype(BF16), jnp.asarray(m1, F32).astype(BF16),
            jnp.asarray(g, F32).astype(BF16))


def _in_proj_kernel(x_ref, nw_ref, wf_ref, wq_ref, wk_ref, wv_ref, wg_ref, wab_ref, fc_ref,
                    wgate_ref, bgate_ref, m1_ref,
                    z1r_ref, z1i_ref, q_ref, k_ref, v_ref, g_ref, laf_ref, lab_ref,
                    zs_r, zs_i, os_r, os_i):
    n1, tb, d = x_ref.shape
    tm = n1 * tb
    tiled = lambda val: val.reshape(n1, tb, val.shape[1])
    h = _rms(x_ref[...].reshape(tm, d), nw_ref[...]).astype(BF16)
    fp = _dot(h, wf_ref[...]).astype(BF16)
    groups = zs_r.shape[0]
    for grp in range(groups):
        zz = _dot(fp[:, grp * LANES:(grp + 1) * LANES], fc_ref[...])
        zs_r[grp] = zz[:, :LANES]
        zs_i[grp] = zz[:, LANES:]
    for bl in range(tb):
        rows = pl.ds(bl, n1, stride=tb)
        zr = jnp.concatenate([zs_r[s, rows, :] for s in range(groups)], axis=1)
        zi = jnp.concatenate([zs_i[s, rows, :] for s in range(groups)], axis=1)
        out = _dot(m1_ref[...], jnp.concatenate([zr, zi], axis=0).astype(BF16))
        for s in range(groups):
            os_r[s, rows, :] = out[:n1, s * LANES:(s + 1) * LANES]
            os_i[s, rows, :] = out[n1:, s * LANES:(s + 1) * LANES]
    for s in range(groups):
        z1r_ref[:, :, s * LANES:(s + 1) * LANES] = tiled(os_r[s])
        z1i_ref[:, :, s * LANES:(s + 1) * LANES] = tiled(os_i[s])
    q_ref[...] = tiled(_dot(h, wq_ref[...]))
    k_ref[...] = tiled(_dot(h, wk_ref[...]))
    v_ref[...] = tiled(_dot(h, wv_ref[...]))
    g_ref[...] = tiled(_dot(h, wg_ref[...]))
    ab = _dot(h, wab_ref[...]).astype(BF16)
    logits = _dot(ab, wgate_ref[...]) + bgate_ref[...]
    la = (jnp.minimum(logits, 0.0) - jnp.log1p(jnp.exp(-jnp.abs(logits)))) * (1.0 / GATE_LOGIT_NORMALIZER)
    d_k = laf_ref.shape[2]
    laf_ref[...] = tiled(la[:, :d_k])
    lab_ref[...] = tiled(la[:, d_k:])


def _in_proj(x4, nw, w_in, w_af, b_af, w_ab, b_ab, fc, m1):
    batch, n1, _, d = x4.shape
    d_f = d // 2
    d_v = d - d_f
    d_k = d_v // 2
    assert d_f // F_GROUPS == LANES
    o = np.cumsum([0, d_f, d_k, d_k, d_v, d_v, GATE_RANK, GATE_RANK])
    wb = w_in.astype(BF16)
    wf, wq, wk, wv, wg = (wb[:, o[i]:o[i + 1]] for i in range(5))
    wab = jnp.pad(wb[:, o[5]:o[7]], ((0, 0), (0, LANES - 2 * GATE_RANK)))
    wgate = jnp.zeros((LANES, 2 * d_k), F32)
    wgate = wgate.at[:GATE_RANK, :d_k].set(w_af).at[GATE_RANK:2 * GATE_RANK, d_k:].set(w_ab).astype(BF16)
    bgate = jnp.concatenate([b_af, b_ab])[None, :]
    consts = [nw[None, :], wf, wq, wk, wv, wg, wab, fc, wgate, bgate, m1]
    widths = [d_f, d_f, d_k, d_k, d_v, d_v, d_k, d_k]
    slab = pltpu.VMEM((F_GROUPS, n1 * TB, LANES), F32)
    return pl.pallas_call(
        _in_proj_kernel,
        grid=(batch * (n1 // TB),),
        in_specs=[_tile_spec(n1, d)] + [_const_spec(a) for a in consts],
        out_specs=[_tile_spec(n1, w) for w in widths],
        out_shape=[jax.ShapeDtypeStruct((batch, n1, n1, w), F32) for w in widths],
        scratch_shapes=[slab, slab, slab, slab],
        compiler_params=pltpu.CompilerParams(dimension_semantics=("parallel",),
                                             vmem_limit_bytes=VMEM_LIMIT),
        name="in_proj",
    )(x4, *consts)


def _gla_chunk(q, k, v, la, s_ref, *, reverse):
    c, dk2 = q.shape
    dv2 = v.shape[1]
    dk, dv = dk2 // 2, dv2 // 2
    ri = lax.broadcasted_iota(jnp.int32, (c, c), 0)
    ci = lax.broadcasted_iota(jnp.int32, (c, c), 1)
    tri = jnp.where((ci >= ri) if reverse else (ci <= ri), 1.0, 0.0).astype(BF16)
    la_hi = la.astype(BF16)
    la_lo = (la - la_hi.astype(F32)).astype(BF16)
    cum = _dot(tri, la_hi) + _dot(tri, la_lo)
    tot = cum[0:1, :] if reverse else cum[c - 1:c, :]
    qe = (q * (jnp.exp(cum) * (dk ** -0.5))).astype(BF16)
    ke = k * jnp.exp(-cum)
    kend = (k * jnp.exp(tot - cum)).astype(BF16)

    lane_k = lax.broadcasted_iota(jnp.int32, (c, dk2), 1)
    ke_bd = jnp.concatenate([jnp.where(lane_k < dk, ke, 0.0), jnp.where(lane_k >= dk, ke, 0.0)],
                            axis=0).astype(BF16)
    att = lax.dot_general(qe, ke_bd, (((1,), (1,)), ((), ())), preferred_element_type=F32)
    ri2 = lax.broadcasted_iota(jnp.int32, (c, 2 * c), 0)
    cj2 = lax.broadcasted_iota(jnp.int32, (c, 2 * c), 1) % c
    keep = (cj2 >= ri2) if reverse else (cj2 <= ri2)
    att = jnp.where(keep, att, 0.0).astype(BF16)

    lane_v = lax.broadcasted_iota(jnp.int32, (c, dv2), 1)
    vb = v.astype(BF16)
    v_bd = jnp.concatenate([jnp.where(lane_v < dv, v, 0.0), jnp.where(lane_v >= dv, v, 0.0)],
                           axis=0).astype(BF16)
    s_prev = s_ref[...]
    o = _dot(att, v_bd) + _dot(qe, s_prev.astype(BF16))

    cs = lax.dot_general(kend, vb, (((0,), (0,)), ((), ())), preferred_element_type=F32)
    dec = jnp.exp(tot)
    dec_hi = dec.astype(BF16).astype(F32)
    dec_lo = dec - dec_hi
    row_k = lax.broadcasted_iota(jnp.int32, (c, dk2), 0)
    dsel = jnp.where(row_k == 0, dec_hi, jnp.where(row_k == 1, dec_lo, 0.0)).astype(BF16)
    ones_rows = jnp.where(lax.broadcasted_iota(jnp.int32, (c, dv2), 0) < 2, 1.0, 0.0).astype(BF16)
    dec_col = lax.dot_general(dsel, ones_rows, (((0,), (0,)), ((), ())),
                              preferred_element_type=F32)
    rk = lax.broadcasted_iota(jnp.int32, (dk2, dv2), 0) // dk
    cv = lax.broadcasted_iota(jnp.int32, (dk2, dv2), 1) // dv
    s_ref[...] = s_prev * dec_col + jnp.where(rk == cv, cs, 0.0)
    return o


def _gla_kernel(qf_ref, kf_ref, vf_ref, lf_ref, qb_ref, kb_ref, vb_ref, lb_ref,
                of_ref, ob_ref, sf_ref, sb_ref):
    @pl.when(pl.program_id(2) == 0)
    def _():
        sf_ref[...] = jnp.zeros_like(sf_ref)
        sb_ref[...] = jnp.zeros_like(sb_ref)

    nc = qf_ref.shape[1] // CHUNK
    for c in range(nc):
        rf = slice(c * CHUNK, (c + 1) * CHUNK)
        of_ref[0, rf, :] = _gla_chunk(qf_ref[0, rf, :], kf_ref[0, rf, :], vf_ref[0, rf, :],
                                      lf_ref[0, rf, :], sf_ref, reverse=False)
        rb = slice((nc - 1 - c) * CHUNK, (nc - c) * CHUNK)
        ob_ref[0, rb, :] = _gla_chunk(qb_ref[0, rb, :], kb_ref[0, rb, :], vb_ref[0, rb, :],
                                      lb_ref[0, rb, :], sb_ref, reverse=True)


def _gla(q, k, v, laf, lab, *, rows):
    b, s, dk_all = q.shape
    dv_all = v.shape[2]
    pairs = GLA_HEADS // 2
    dk2, dv2 = dk_all // pairs, dv_all // pairs
    nblk = s // rows
    fwd = lambda w: pl.BlockSpec((1, rows, w), lambda bi, p, j: (bi, j, p))
    bwd = lambda w: pl.BlockSpec((1, rows, w), lambda bi, p, j: (bi, nblk - 1 - j, p))
    return pl.pallas_call(
        _gla_kernel,
        grid=(b, pairs, nblk),
        in_specs=[fwd(dk2), fwd(dk2), fwd(dv2), fwd(dk2), bwd(dk2), bwd(dk2), bwd(dv2), bwd(dk2)],
        out_specs=[fwd(dv2), bwd(dv2)],
        out_shape=[jax.ShapeDtypeStruct((b, s, dv_all), F32)] * 2,
        scratch_shapes=[pltpu.VMEM((dk2, dv2), F32), pltpu.VMEM((dk2, dv2), F32)],
        compiler_params=pltpu.CompilerParams(dimension_semantics=("parallel", "parallel", "arbitrary"),
                                             vmem_limit_bytes=VMEM_LIMIT),
        name="gla",
    )(q, k, v, laf, q, k, v, lab)


def _post_kernel(x_ref, z1r_ref, z1i_ref, of_ref, ob_ref, g_ref, gt_ref, onw_ref, wo_f_ref, wo_g_ref,
                 npost_ref, nffn_ref, wgt_ref, wup_ref, wdn_ref, nffn_post_ref, o_ref, ys, *, dv, ff_chunk):
    n2, tb, d = x_ref.shape
    tm = n2 * tb
    flat = lambda ref: ref[...].reshape(tm, ref.shape[2])
    groups = ys.shape[0]
    for kl in range(tb):
        rows = slice(kl * n2, (kl + 1) * n2)
        rhs = jnp.concatenate([z1r_ref[rows, :], z1i_ref[rows, :]], axis=0).astype(BF16)
        y = _dot(gt_ref[kl], rhs)
        for s in range(groups):
            ys[s, pl.ds(kl, n2, stride=tb), :] = y[:, s * LANES:(s + 1) * LANES]
    yf = jnp.concatenate([ys[s] for s in range(groups)], axis=1).astype(BF16)

    x = flat(x_ref)
    o = flat(of_ref) + flat(ob_ref)
    g = flat(g_ref)
    onw = onw_ref[...]
    yg = []
    for hd in range(o.shape[1] // dv):
        sl = slice(hd * dv, (hd + 1) * dv)
        gh = g[:, sl]
        yg.append((_rms(o[:, sl], onw) * (gh * jax.nn.sigmoid(gh))).astype(BF16))
    yg = jnp.concatenate(yg, axis=1)
    m = _dot(yf, wo_f_ref[...]) + _dot(yg, wo_g_ref[...])
    x = x + _rms(m, npost_ref[...])

    h2 = _rms(x, nffn_ref[...]).astype(BF16)
    d_ff = wgt_ref.shape[1]
    f = None
    for c0 in range(0, d_ff, ff_chunk):
        sl = slice(c0, c0 + ff_chunk)
        gt = _dot(h2, wgt_ref[:, sl])
        up = _dot(h2, wup_ref[:, sl])
        act = (gt * jax.nn.sigmoid(gt) * up).astype(BF16)
        part = _dot(act, wdn_ref[sl, :])
        f = part if f is None else f + part
    o_ref[...] = (x + _rms(f, nffn_post_ref[...])).reshape(n2, tb, d)


def _post(x4, z1r, z1i, of, ob, g, gtab, onw, w_out, npost, nffn, w_gate, w_up, w_down, nffn_post, *, ff_chunk):
    batch, n1, _, d = x4.shape
    d_f = z1r.shape[3]
    d_v = of.shape[3]
    dv = d_v // GLA_HEADS
    nb = n1 // TB
    wo = w_out.astype(BF16)
    vec = lambda a: a[None, :]
    consts = [vec(onw), wo[:d_f], wo[d_f:], vec(npost), vec(nffn), w_gate.astype(BF16), w_up.astype(BF16),
              w_down.astype(BF16), vec(nffn_post)]
    z_spec = pl.BlockSpec((None, TB * n1, d_f), lambda i: (i // nb, i % nb, 0))
    z3 = lambda z: z.reshape(batch, n1 * n1, d_f)
    return pl.pallas_call(
        functools.partial(_post_kernel, dv=dv, ff_chunk=ff_chunk),
        grid=(batch * nb,),
        in_specs=[_tile_spec(n1, d), z_spec, z_spec, _tile_spec(n1, d_v), _tile_spec(n1, d_v),
                  _tile_spec(n1, d_v), pl.BlockSpec((TB, n1, 2 * n1), lambda i: (i % nb, 0, 0))]
                 + [_const_spec(a) for a in consts],
        out_specs=_tile_spec(n1, d),
        out_shape=jax.ShapeDtypeStruct(x4.shape, F32),
        scratch_shapes=[pltpu.VMEM((F_GROUPS, n1 * TB, LANES), F32)],
        compiler_params=pltpu.CompilerParams(dimension_semantics=("parallel",),
                                             vmem_limit_bytes=VMEM_LIMIT),
        name="post",
    )(x4, z3(z1r), z3(z1i), of, ob, g, gtab, *consts)


def kernel(x, norm_mix_pre, w_in, w_alpha_fwd, b_alpha_fwd, w_alpha_bwd, b_alpha_bwd, gla_out_norm,
           w_out, norm_mix_post, norm_ffn_pre, w_ffn_gate, w_ffn_up, w_ffn_down, norm_ffn_post):
    batch, seq, d = x.shape
    depth = w_in.shape[0]
    fc, m1, gtab = _dft_constants(seq, d // 2 // F_GROUPS)
    n1 = m1.shape[0] // 2
    x4 = x.reshape(batch, n1, n1, d)
    seq3 = lambda a: a.reshape(batch, seq, a.shape[3])
    tile4 = lambda a: a.reshape(batch, n1, n1, a.shape[2])
    for l in range(depth):
        z1r, z1i, q, k, v, g, laf, lab = _in_proj(
            x4, norm_mix_pre[l], w_in[l], w_alpha_fwd[l], b_alpha_fwd[l], w_alpha_bwd[l], b_alpha_bwd[l],
            fc, m1)
        of, ob = _gla(seq3(q), seq3(k), seq3(v), seq3(laf), seq3(lab), rows=512)
        x4 = _post(x4, z1r, z1i, tile4(of), tile4(ob), g, gtab, gla_out_norm[l], w_out[l], norm_mix_post[l],
                   norm_ffn_pre[l], w_ffn_gate[l], w_ffn_up[l], w_ffn_down[l], norm_ffn_post[l],
                   ff_chunk=256)
    return x4.reshape(batch, seq, d)
```

```python
import functools

import numpy as np
import jax
import jax.numpy as jnp
from jax import lax
from jax.experimental import pallas as pl
from jax.experimental.pallas import tpu as pltpu

F_GROUPS = 4
GLA_HEADS = 4
GATE_RANK = 16
GATE_LOGIT_NORMALIZER = 16.0
CHUNK = 64
EPS = 1e-6

LANES = 128
SUBLANES = 8
TB = SUBLANES
VMEM_LIMIT = 56 * 1024 * 1024

BF16 = jnp.bfloat16
F32 = jnp.float32


def _rms(x, w):
    return x * lax.rsqrt(jnp.mean(x * x, axis=-1, keepdims=True) + EPS) * w


def _dot(a, b):
    return jnp.dot(a, b, preferred_element_type=F32)


def _const_spec(a):
    return pl.BlockSpec(a.shape, lambda *_: (0,) * a.ndim, pipeline_mode=pl.Buffered(1))


def _tile_spec(n1, width):
    nb = n1 // TB
    return pl.BlockSpec((None, n1, TB, width), lambda i: (i // nb, 0, i % nb, 0))


def _dft_constants(seq, group_dim):
    n1 = int(round(seq ** 0.5))
    assert n1 * n1 == seq
    c = np.arange(group_dim)
    ang = 2.0 * np.pi * ((c[:, None] * c[None, :]) % group_dim) / group_dim
    fc = np.concatenate([np.cos(ang), -np.sin(ang)], axis=1) / np.sqrt(group_dim)
    a = np.arange(n1)
    ang1 = 2.0 * np.pi * ((a[:, None] * a[None, :]) % n1) / n1
    c1, s1 = np.cos(ang1), np.sin(ang1)
    m1 = np.block([[c1, s1], [-s1, c1]]) / n1
    k1 = np.arange(n1)[:, None, None]
    k2 = np.arange(n1)[None, :, None]
    b = np.arange(n1)[None, None, :]
    ang2 = 2.0 * np.pi * ((b * (k1 + n1 * k2)) % seq) / seq
    g = np.concatenate([np.cos(ang2), np.sin(ang2)], axis=2)
    return (jnp.asarray(fc, F32).astype(BF16), jnp.asarray(m1, F32).astype(BF16),
            jnp.asarray(g, F32).astype(BF16))


def _in_proj_kernel(x_ref, nw_ref, wf_ref, wq_ref, wk_ref, wv_ref, wg_ref, wab_ref, fc_ref,
                    wgate_ref, bgate_ref, m1_ref,
                    z1r_ref, z1i_ref, q_ref, k_ref, v_ref, g_ref, laf_ref, lab_ref,
                    zs_r, zs_i, os_r, os_i):
    n1, tb, d = x_ref.shape
    tm = n1 * tb
    tiled = lambda val: val.reshape(n1, tb, val.shape[1])
    h = _rms(x_ref[...].reshape(tm, d), nw_ref[...]).astype(BF16)
    fp = _dot(h, wf_ref[...]).astype(BF16)
    groups = zs_r.shape[0]
    for grp in range(groups):
        zz = _dot(fp[:, grp * LANES:(grp + 1) * LANES], fc_ref[...])
        zs_r[grp] = zz[:, :LANES]
        zs_i[grp] = zz[:, LANES:]
    for bl in range(tb):
        rows = pl.ds(bl, n1, stride=tb)
        zr = jnp.concatenate([zs_r[s, rows, :] for s in range(groups)], axis=1)
        zi = jnp.concatenate([zs_i[s, rows, :] for s in range(groups)], axis=1)
        out = _dot(m1_ref[...], jnp.concatenate([zr, zi], axis=0).astype(BF16))
        for s in range(groups):
            os_r[s, rows, :] = out[:n1, s * LANES:(s + 1) * LANES]
            os_i[s, rows, :] = out[n1:, s * LANES:(s + 1) * LANES]
    for s in range(groups):
        z1r_ref[:, :, s * LANES:(s + 1) * LANES] = tiled(os_r[s])
        z1i_ref[:, :, s * LANES:(s + 1) * LANES] = tiled(os_i[s])
    q_ref[...] = tiled(_dot(h, wq_ref[...]))
    k_ref[...] = tiled(_dot(h, wk_ref[...]))
    v_ref[...] = tiled(_dot(h, wv_ref[...]))
    g_ref[...] = tiled(_dot(h, wg_ref[...]))
    ab = _dot(h, wab_ref[...]).astype(BF16)
    logits = _dot(ab, wgate_ref[...]) + bgate_ref[...]
    la = (jnp.minimum(logits, 0.0) - jnp.log1p(jnp.exp(-jnp.abs(logits)))) * (1.0 / GATE_LOGIT_NORMALIZER)
    d_k = laf_ref.shape[2]
    laf_ref[...] = tiled(la[:, :d_k])
    lab_ref[...] = tiled(la[:, d_k:])


def _in_proj(x4, nw, w_in, w_af, b_af, w_ab, b_ab, fc, m1):
    batch, n1, _, d = x4.shape
    d_f = d // 2
    d_v = d - d_f
    d_k = d_v // 2
    assert d_f // F_GROUPS == LANES
    o = np.cumsum([0, d_f, d_k, d_k, d_v, d_v, GATE_RANK, GATE_RANK])
    wb = w_in.astype(BF16)
    wf, wq, wk, wv, wg = (wb[:, o[i]:o[i + 1]] for i in range(5))
    wab = jnp.pad(wb[:, o[5]:o[7]], ((0, 0), (0, LANES - 2 * GATE_RANK)))
    wgate = jnp.zeros((LANES, 2 * d_k), F32)
    wgate = wgate.at[:GATE_RANK, :d_k].set(w_af).at[GATE_RANK:2 * GATE_RANK, d_k:].set(w_ab).astype(BF16)
    bgate = jnp.concatenate([b_af, b_ab])[None, :]
    consts = [nw[None, :], wf, wq, wk, wv, wg, wab, fc, wgate, bgate, m1]
    widths = [d_f, d_f, d_k, d_k, d_v, d_v, d_k, d_k]
    slab = pltpu.VMEM((F_GROUPS, n1 * TB, LANES), F32)
    return pl.pallas_call(
        _in_proj_kernel,
        grid=(batch * (n1 // TB),),
        in_specs=[_tile_spec(n1, d)] + [_const_spec(a) for a in consts],
        out_specs=[_tile_spec(n1, w) for w in widths],
        out_shape=[jax.ShapeDtypeStruct((batch, n1, n1, w), F32) for w in widths],
        scratch_shapes=[slab, slab, slab, slab],
        compiler_params=pltpu.CompilerParams(dimension_semantics=("parallel",),
                                             vmem_limit_bytes=VMEM_LIMIT),
        name="in_proj",
    )(x4, *consts)


def _gla_constants(c, dk, dv):
    dk2, dv2 = 2 * dk, 2 * dv
    ri = lax.broadcasted_iota(jnp.int32, (c, 2 * c), 0)
    cj = lax.broadcasted_iota(jnp.int32, (c, 2 * c), 1) % c
    lane_k = lax.broadcasted_iota(jnp.int32, (c, dk2), 1)
    lane_v = lax.broadcasted_iota(jnp.int32, (c, dv2), 1)
    sel_r = lax.broadcasted_iota(jnp.int32, (2 * SUBLANES, dk2), 0)
    sel_l = lax.broadcasted_iota(jnp.int32, (2 * SUBLANES, dk2), 1)
    xs_r = lax.broadcasted_iota(jnp.int32, (2 * SUBLANES, dv2), 0)
    xs_l = lax.broadcasted_iota(jnp.int32, (2 * SUBLANES, dv2), 1)
    rk = lax.broadcasted_iota(jnp.int32, (dk2, dv2), 0) // dk
    cv = lax.broadcasted_iota(jnp.int32, (dk2, dv2), 1) // dv
    own = (sel_r < 4) & ((sel_l < dk) == (sel_r < 2))
    xsel = jnp.where(((xs_r < 2) & (xs_l >= dv)) | ((xs_r >= 2) & (xs_r < 4) & (xs_l < dv)), 1.0, 0.0)
    return dict(lower=cj <= ri, upper=cj >= ri, k_lo=lane_k < dk, v_lo=lane_v < dv,
                sel_hi=own & (sel_r % 2 == 0), sel_lo=own & (sel_r % 2 == 1),
                xsel=xsel.astype(BF16), diag=rk == cv)


def _gla_block(q_ref, k_ref, v_ref, la_ref, o_ref, s_ref, cst, *, reverse):
    c = CHUNK
    dk = q_ref.shape[2] // 2
    dv = v_ref.shape[2] // 2
    nc = q_ref.shape[1] // c
    chunks = range(nc)
    rows = [slice(i * c, (i + 1) * c) for i in chunks]
    keep = cst["upper"] if reverse else cst["lower"]
    tri2 = jnp.where(keep, 1.0, 0.0).astype(BF16)
    per_head = lambda a, lo: jnp.concatenate([jnp.where(lo, a, 0.0), jnp.where(lo, 0.0, a)], axis=0).astype(BF16)

    cum = []
    for i in chunks:
        la = la_ref[0, rows[i], :]
        la_hi = la.astype(BF16)
        la_lo = (la - la_hi.astype(F32)).astype(BF16)
        cum.append(_dot(tri2, jnp.concatenate([la_hi, la_lo], axis=0)))
    tot = [cm[0:1, :] if reverse else cm[c - 1:c, :] for cm in cum]
    dec = [jnp.exp(t) for t in tot]
    qe = [(q_ref[0, rows[i], :] * (jnp.exp(cum[i]) * (dk ** -0.5))).astype(BF16) for i in chunks]
    ke = [k_ref[0, rows[i], :] * jnp.exp(-cum[i]) for i in chunks]
    att = [lax.dot_general(qe[i], per_head(ke[i], cst["k_lo"]), (((1,), (1,)), ((), ())),
                           preferred_element_type=F32) for i in chunks]
    att = [jnp.where(keep, a, 0.0).astype(BF16) for a in att]
    v_bd = [per_head(v_ref[0, rows[i], :], cst["v_lo"]) for i in chunks]
    o_intra = [_dot(att[i], v_bd[i]) for i in chunks]

    m = []
    for i in chunks:
        dec_hi = dec[i].astype(BF16).astype(F32)
        dsel = jnp.where(cst["sel_hi"], dec_hi, jnp.where(cst["sel_lo"], dec[i] - dec_hi, 0.0)).astype(BF16)
        m.append(lax.dot_general(jnp.concatenate([per_head(ke[i] * dec[i], cst["k_lo"]), dsel], axis=0),
                                 jnp.concatenate([v_bd[i], cst["xsel"]], axis=0),
                                 (((0,), (0,)), ((), ())), preferred_element_type=F32))

    s = s_ref[...]
    for i in (reversed(chunks) if reverse else chunks):
        o_ref[0, rows[i], :] = o_intra[i] + _dot(qe[i], s.astype(BF16))
        dec_col = jnp.concatenate([m[i][:, dv:], m[i][:, :dv]], axis=1)
        s = s * dec_col + jnp.where(cst["diag"], m[i], 0.0)
    s_ref[...] = s


def _gla_kernel(qf_ref, kf_ref, vf_ref, lf_ref, qb_ref, kb_ref, vb_ref, lb_ref,
                of_ref, ob_ref, sf_ref, sb_ref):
    @pl.when(pl.program_id(2) == 0)
    def _():
        sf_ref[...] = jnp.zeros_like(sf_ref)
        sb_ref[...] = jnp.zeros_like(sb_ref)

    cst = _gla_constants(CHUNK, qf_ref.shape[2] // 2, vf_ref.shape[2] // 2)
    _gla_block(qf_ref, kf_ref, vf_ref, lf_ref, of_ref, sf_ref, cst, reverse=False)
    _gla_block(qb_ref, kb_ref, vb_ref, lb_ref, ob_ref, sb_ref, cst, reverse=True)


def _gla(q, k, v, laf, lab, *, rows):
    b, s, dk_all = q.shape
    dv_all = v.shape[2]
    pairs = GLA_HEADS // 2
    dk2, dv2 = dk_all // pairs, dv_all // pairs
    nblk = s // rows
    fwd = lambda w: pl.BlockSpec((1, rows, w), lambda bi, p, j: (bi, j, p))
    bwd = lambda w: pl.BlockSpec((1, rows, w), lambda bi, p, j: (bi, nblk - 1 - j, p))
    return pl.pallas_call(
        _gla_kernel,
        grid=(b, pairs, nblk),
        in_specs=[fwd(dk2), fwd(dk2), fwd(dv2), fwd(dk2), bwd(dk2), bwd(dk2), bwd(dv2), bwd(dk2)],
        out_specs=[fwd(dv2), bwd(dv2)],
        out_shape=[jax.ShapeDtypeStruct((b, s, dv_all), F32)] * 2,
        scratch_shapes=[pltpu.VMEM((dk2, dv2), F32), pltpu.VMEM((dk2, dv2), F32)],
        compiler_params=pltpu.CompilerParams(dimension_semantics=("parallel", "parallel", "arbitrary"),
                                             vmem_limit_bytes=VMEM_LIMIT),
        name="gla",
    )(q, k, v, laf, q, k, v, lab)


def _post_kernel(x_ref, z1r_ref, z1i_ref, of_ref, ob_ref, g_ref, gt_ref, onw_ref, wo_f_ref, wo_g_ref,
                 npost_ref, nffn_ref, wgt_ref, wup_ref, wdn_ref, nffn_post_ref, o_ref, ys, *, dv, ff_chunk):
    n2, tb, d = x_ref.shape
    tm = n2 * tb
    flat = lambda ref: ref[...].reshape(tm, ref.shape[2])
    groups = ys.shape[0]
    for kl in range(tb):
        rows = slice(kl * n2, (kl + 1) * n2)
        rhs = jnp.concatenate([z1r_ref[rows, :], z1i_ref[rows, :]], axis=0).astype(BF16)
        y = _dot(gt_ref[kl], rhs)
        for s in range(groups):
            ys[s, pl.ds(kl, n2, stride=tb), :] = y[:, s * LANES:(s + 1) * LANES]
    yf = jnp.concatenate([ys[s] for s in range(groups)], axis=1).astype(BF16)

    x = flat(x_ref)
    o = flat(of_ref) + flat(ob_ref)
    g = flat(g_ref)
    onw = onw_ref[...]
    yg = []
    for hd in range(o.shape[1] // dv):
        sl = slice(hd * dv, (hd + 1) * dv)
        gh = g[:, sl]
        yg.append((_rms(o[:, sl], onw) * (gh * jax.nn.sigmoid(gh))).astype(BF16))
    yg = jnp.concatenate(yg, axis=1)
    m = _dot(yf, wo_f_ref[...]) + _dot(yg, wo_g_ref[...])
    x = x + _rms(m, npost_ref[...])

    h2 = _rms(x, nffn_ref[...]).astype(BF16)
    d_ff = wgt_ref.shape[1]
    f = None
    for c0 in range(0, d_ff, ff_chunk):
        sl = slice(c0, c0 + ff_chunk)
        gt = _dot(h2, wgt_ref[:, sl])
        up = _dot(h2, wup_ref[:, sl])
        act = (gt * jax.nn.sigmoid(gt) * up).astype(BF16)
        part = _dot(act, wdn_ref[sl, :])
        f = part if f is None else f + part
    o_ref[...] = (x + _rms(f, nffn_post_ref[...])).reshape(n2, tb, d)


def _post(x4, z1r, z1i, of, ob, g, gtab, onw, w_out, npost, nffn, w_gate, w_up, w_down, nffn_post, *, ff_chunk):
    batch, n1, _, d = x4.shape
    d_f = z1r.shape[3]
    d_v = of.shape[3]
    dv = d_v // GLA_HEADS
    nb = n1 // TB
    wo = w_out.astype(BF16)
    vec = lambda a: a[None, :]
    consts = [vec(onw), wo[:d_f], wo[d_f:], vec(npost), vec(nffn), w_gate.astype(BF16), w_up.astype(BF16),
              w_down.astype(BF16), vec(nffn_post)]
    z_spec = pl.BlockSpec((None, TB * n1, d_f), lambda i: (i // nb, i % nb, 0))
    z3 = lambda z: z.reshape(batch, n1 * n1, d_f)
    return pl.pallas_call(
        functools.partial(_post_kernel, dv=dv, ff_chunk=ff_chunk),
        grid=(batch * nb,),
        in_specs=[_tile_spec(n1, d), z_spec, z_spec, _tile_spec(n1, d_v), _tile_spec(n1, d_v),
                  _tile_spec(n1, d_v), pl.BlockSpec((TB, n1, 2 * n1), lambda i: (i % nb, 0, 0))]
                 + [_const_spec(a) for a in consts],
        out_specs=_tile_spec(n1, d),
        out_shape=jax.ShapeDtypeStruct(x4.shape, F32),
        scratch_shapes=[pltpu.VMEM((F_GROUPS, n1 * TB, LANES), F32)],
        compiler_params=pltpu.CompilerParams(dimension_semantics=("parallel",),
                                             vmem_limit_bytes=VMEM_LIMIT),
        name="post",
    )(x4, z3(z1r), z3(z1i), of, ob, g, gtab, *consts)


def kernel(x, norm_mix_pre, w_in, w_alpha_fwd, b_alpha_fwd, w_alpha_bwd, b_alpha_bwd, gla_out_norm,
           w_out, norm_mix_post, norm_ffn_pre, w_ffn_gate, w_ffn_up, w_ffn_down, norm_ffn_post):
    batch, seq, d = x.shape
    depth = w_in.shape[0]
    fc, m1, gtab = _dft_constants(seq, d // 2 // F_GROUPS)
    n1 = m1.shape[0] // 2
    x4 = x.reshape(batch, n1, n1, d)
    seq3 = lambda a: a.reshape(batch, seq, a.shape[3])
    tile4 = lambda a: a.reshape(batch, n1, n1, a.shape[2])
    for l in range(depth):
        z1r, z1i, q, k, v, g, laf, lab = _in_proj(
            x4, norm_mix_pre[l], w_in[l], w_alpha_fwd[l], b_alpha_fwd[l], w_alpha_bwd[l], b_alpha_bwd[l],
            fc, m1)
        of, ob = _gla(seq3(q), seq3(k), seq3(v), seq3(laf), seq3(lab), rows=1024)
        x4 = _post(x4, z1r, z1i, tile4(of), tile4(ob), g, gtab, gla_out_norm[l], w_out[l], norm_mix_post[l],
                   norm_ffn_pre[l], w_ffn_gate[l], w_ffn_up[l], w_ffn_down[l], norm_ffn_post[l],
                   ff_chunk=256)
    return x4.reshape(batch, seq, d)
```

```python
import functools

import numpy as np
import jax
import jax.numpy as jnp
from jax import lax
from jax.experimental import pallas as pl
from jax.experimental.pallas import tpu as pltpu

F_GROUPS = 4
GLA_HEADS = 4
GATE_RANK = 16
GATE_LOGIT_NORMALIZER = 16.0
CHUNK = 64
EPS = 1e-6

LANES = 128
SUBLANES = 8
TB = SUBLANES
VMEM_LIMIT = 56 * 1024 * 1024

BF16 = jnp.bfloat16
F32 = jnp.float32


def _rms(x, w):
    return x * lax.rsqrt(jnp.mean(x * x, axis=-1, keepdims=True) + EPS) * w


def _dot(a, b):
    return jnp.dot(a, b, preferred_element_type=F32)


def _const_spec(a):
    return pl.BlockSpec(a.shape, lambda *_: (0,) * a.ndim, pipeline_mode=pl.Buffered(1))


def _tile_spec(n1, width):
    nb = n1 // TB
    return pl.BlockSpec((None, n1, TB, width), lambda i: (i // nb, 0, i % nb, 0))


def _dft_constants(seq, group_dim):
    n1 = int(round(seq ** 0.5))
    assert n1 * n1 == seq
    c = np.arange(group_dim)
    ang = 2.0 * np.pi * ((c[:, None] * c[None, :]) % group_dim) / group_dim
    fc = np.concatenate([np.cos(ang), -np.sin(ang)], axis=1) / np.sqrt(group_dim)
    a = np.arange(n1)
    ang1 = 2.0 * np.pi * ((a[:, None] * a[None, :]) % n1) / n1
    c1, s1 = np.cos(ang1), np.sin(ang1)
    m1 = np.block([[c1, s1], [-s1, c1]]) / n1
    k1 = np.arange(n1)[:, None, None]
    k2 = np.arange(n1)[None, :, None]
    b = np.arange(n1)[None, None, :]
    ang2 = 2.0 * np.pi * ((b * (k1 + n1 * k2)) % seq) / seq
    g = np.concatenate([np.cos(ang2), np.sin(ang2)], axis=2)
    return (jnp.asarray(fc, F32).astype(BF16), jnp.asarray(m1, F32).astype(BF16),
            jnp.asarray(g, F32).astype(BF16))


def _in_proj_kernel(x_ref, nw_ref, wf_ref, wq_ref, wk_ref, wv_ref, wg_ref, wab_ref, fc_ref,
                    wgate_ref, bgate_ref, m1_ref,
                    z1r_ref, z1i_ref, q_ref, k_ref, v_ref, g_ref, laf_ref, lab_ref,
                    zs_r, zs_i, os_r, os_i, *, n_sub):
    n1, tb, d = x_ref.shape
    groups = zs_r.shape[0]
    d_k = laf_ref.shape[2]
    sub_n1 = n1 // n_sub
    sub_rows = sub_n1 * tb
    for i in range(n_sub):
        asl = slice(i * sub_n1, (i + 1) * sub_n1)
        rsl = slice(i * sub_rows, (i + 1) * sub_rows)
        tiled = lambda val: val.reshape(sub_n1, tb, val.shape[1])
        h = _rms(x_ref[asl].reshape(sub_rows, d), nw_ref[...]).astype(BF16)
        ab = _dot(h, wab_ref[...]).astype(BF16)
        logits = _dot(ab, wgate_ref[...]) + bgate_ref[...]
        la = (jnp.minimum(logits, 0.0) - jnp.log1p(jnp.exp(-jnp.abs(logits)))) * (1.0 / GATE_LOGIT_NORMALIZER)
        laf_ref[asl] = tiled(la[:, :d_k])
        lab_ref[asl] = tiled(la[:, d_k:])
        fp = _dot(h, wf_ref[...]).astype(BF16)
        for grp in range(groups):
            zz = _dot(fp[:, grp * LANES:(grp + 1) * LANES], fc_ref[...])
            zs_r[grp, rsl, :] = zz[:, :LANES]
            zs_i[grp, rsl, :] = zz[:, LANES:]
        q_ref[asl] = tiled(_dot(h, wq_ref[...]))
        k_ref[asl] = tiled(_dot(h, wk_ref[...]))
        v_ref[asl] = tiled(_dot(h, wv_ref[...]))
        g_ref[asl] = tiled(_dot(h, wg_ref[...]))
    tiled = lambda val: val.reshape(n1, tb, val.shape[1])
    for bl in range(tb):
        rows = pl.ds(bl, n1, stride=tb)
        zr = jnp.concatenate([zs_r[s, rows, :] for s in range(groups)], axis=1)
        zi = jnp.concatenate([zs_i[s, rows, :] for s in range(groups)], axis=1)
        out = _dot(m1_ref[...], jnp.concatenate([zr, zi], axis=0).astype(BF16))
        for s in range(groups):
            os_r[s, rows, :] = out[:n1, s * LANES:(s + 1) * LANES]
            os_i[s, rows, :] = out[n1:, s * LANES:(s + 1) * LANES]
    for s in range(groups):
        z1r_ref[:, :, s * LANES:(s + 1) * LANES] = tiled(os_r[s])
        z1i_ref[:, :, s * LANES:(s + 1) * LANES] = tiled(os_i[s])


def _in_proj(x4, nw, w_in, w_af, b_af, w_ab, b_ab, fc, m1):
    batch, n1, _, d = x4.shape
    d_f = d // 2
    d_v = d - d_f
    d_k = d_v // 2
    assert d_f // F_GROUPS == LANES
    o = np.cumsum([0, d_f, d_k, d_k, d_v, d_v, GATE_RANK, GATE_RANK])
    wb = w_in.astype(BF16)
    wf, wq, wk, wv, wg = (wb[:, o[i]:o[i + 1]] for i in range(5))
    wab = jnp.pad(wb[:, o[5]:o[7]], ((0, 0), (0, LANES - 2 * GATE_RANK)))
    wgate = jnp.zeros((LANES, 2 * d_k), F32)
    wgate = wgate.at[:GATE_RANK, :d_k].set(w_af).at[GATE_RANK:2 * GATE_RANK, d_k:].set(w_ab).astype(BF16)
    bgate = jnp.concatenate([b_af, b_ab])[None, :]
    consts = [nw[None, :], wf, wq, wk, wv, wg, wab, fc, wgate, bgate, m1]
    widths = [d_f, d_f, d_k, d_k, d_v, d_v, d_k, d_k]
    slab = pltpu.VMEM((F_GROUPS, n1 * TB, LANES), F32)
    return pl.pallas_call(
        functools.partial(_in_proj_kernel, n_sub=2),
        grid=(batch * (n1 // TB),),
        in_specs=[_tile_spec(n1, d)] + [_const_spec(a) for a in consts],
        out_specs=[_tile_spec(n1, w) for w in widths],
        out_shape=[jax.ShapeDtypeStruct((batch, n1, n1, w), F32) for w in widths],
        scratch_shapes=[slab, slab, slab, slab],
        compiler_params=pltpu.CompilerParams(dimension_semantics=("parallel",),
                                             vmem_limit_bytes=VMEM_LIMIT),
        name="in_proj",
    )(x4, *consts)


def _gla_constants(c, dk, dv):
    dk2, dv2 = 2 * dk, 2 * dv
    ri = lax.broadcasted_iota(jnp.int32, (c, 2 * c), 0)
    cj = lax.broadcasted_iota(jnp.int32, (c, 2 * c), 1) % c
    lane_k = lax.broadcasted_iota(jnp.int32, (c, dk2), 1)
    lane_v = lax.broadcasted_iota(jnp.int32, (c, dv2), 1)
    sel_r = lax.broadcasted_iota(jnp.int32, (2 * SUBLANES, dk2), 0)
    sel_l = lax.broadcasted_iota(jnp.int32, (2 * SUBLANES, dk2), 1)
    xs_r = lax.broadcasted_iota(jnp.int32, (2 * SUBLANES, dv2), 0)
    xs_l = lax.broadcasted_iota(jnp.int32, (2 * SUBLANES, dv2), 1)
    rk = lax.broadcasted_iota(jnp.int32, (dk2, dv2), 0) // dk
    cv = lax.broadcasted_iota(jnp.int32, (dk2, dv2), 1) // dv
    own = (sel_r < 4) & ((sel_l < dk) == (sel_r < 2))
    xsel = jnp.where(((xs_r < 2) & (xs_l >= dv)) | ((xs_r >= 2) & (xs_r < 4) & (xs_l < dv)), 1.0, 0.0)
    return dict(lower=cj <= ri, upper=cj >= ri, k_lo=lane_k < dk, v_lo=lane_v < dv,
                sel_hi=own & (sel_r % 2 == 0), sel_lo=own & (sel_r % 2 == 1),
                xsel=xsel.astype(BF16), diag=rk == cv)


def _gla_block(q_ref, k_ref, v_ref, la_ref, o_ref, s_ref, cst, *, start, n_rows, reverse, accumulate):
    c = CHUNK
    dk = q_ref.shape[2] // 2
    dv = v_ref.shape[2] // 2
    chunks = range(n_rows // c)
    rows = [pl.ds(pl.multiple_of(start + i * c, c), c) for i in chunks]
    keep = cst["upper"] if reverse else cst["lower"]
    tri2 = jnp.where(keep, 1.0, 0.0).astype(BF16)
    per_head = lambda a, lo: jnp.concatenate([jnp.where(lo, a, 0.0), jnp.where(lo, 0.0, a)], axis=0).astype(BF16)

    cum = []
    for i in chunks:
        la = la_ref[0, rows[i], :]
        la_hi = la.astype(BF16)
        la_lo = (la - la_hi.astype(F32)).astype(BF16)
        cum.append(_dot(tri2, jnp.concatenate([la_hi, la_lo], axis=0)))
    tot = [cm[0:1, :] if reverse else cm[c - 1:c, :] for cm in cum]
    dec = [jnp.exp(t) for t in tot]
    qe = [(q_ref[0, rows[i], :] * (jnp.exp(cum[i]) * (dk ** -0.5))).astype(BF16) for i in chunks]
    ke = [k_ref[0, rows[i], :] * jnp.exp(-cum[i]) for i in chunks]
    att = [lax.dot_general(qe[i], per_head(ke[i], cst["k_lo"]), (((1,), (1,)), ((), ())),
                           preferred_element_type=F32) for i in chunks]
    att = [jnp.where(keep, a, 0.0).astype(BF16) for a in att]
    v_bd = [per_head(v_ref[0, rows[i], :], cst["v_lo"]) for i in chunks]
    o_intra = [_dot(att[i], v_bd[i]) for i in chunks]

    m = []
    for i in chunks:
        dec_hi = dec[i].astype(BF16).astype(F32)
        dsel = jnp.where(cst["sel_hi"], dec_hi, jnp.where(cst["sel_lo"], dec[i] - dec_hi, 0.0)).astype(BF16)
        m.append(lax.dot_general(jnp.concatenate([per_head(ke[i] * dec[i], cst["k_lo"]), dsel], axis=0),
                                 jnp.concatenate([v_bd[i], cst["xsel"]], axis=0),
                                 (((0,), (0,)), ((), ())), preferred_element_type=F32))

    s = s_ref[...]
    for i in (reversed(chunks) if reverse else chunks):
        o = o_intra[i] + _dot(qe[i], s.astype(BF16))
        o_ref[0, rows[i], :] = o_ref[0, rows[i], :] + o if accumulate else o
        dec_col = jnp.concatenate([m[i][:, dv:], m[i][:, :dv]], axis=1)
        s = s * dec_col + jnp.where(cst["diag"], m[i], 0.0)
    s_ref[...] = s


def _gla_kernel(q_ref, k_ref, v_ref, lf_ref, lb_ref, o_ref, sf_ref, sb_ref, *, n_rows):
    sf_ref[...] = jnp.zeros_like(sf_ref)
    sb_ref[...] = jnp.zeros_like(sb_ref)
    cst = _gla_constants(CHUNK, q_ref.shape[2] // 2, v_ref.shape[2] // 2)
    nblk = q_ref.shape[1] // n_rows

    def sweep(accumulate):
        def body(j, carry):
            _gla_block(q_ref, k_ref, v_ref, lf_ref, o_ref, sf_ref, cst, start=j * n_rows, n_rows=n_rows,
                       reverse=False, accumulate=accumulate)
            _gla_block(q_ref, k_ref, v_ref, lb_ref, o_ref, sb_ref, cst, start=(nblk - 1 - j) * n_rows,
                       n_rows=n_rows, reverse=True, accumulate=accumulate)
            return carry
        return body

    lax.fori_loop(0, nblk // 2, sweep(False), 0)
    lax.fori_loop(nblk // 2, nblk, sweep(True), 0)


def _gla(q, k, v, laf, lab, *, n_rows):
    b, s, dk_all = q.shape
    dv_all = v.shape[2]
    pairs = GLA_HEADS // 2
    dk2, dv2 = dk_all // pairs, dv_all // pairs
    assert s % (2 * n_rows) == 0
    spec = lambda w: pl.BlockSpec((1, s, w), lambda bi, p: (bi, 0, p))
    return pl.pallas_call(
        functools.partial(_gla_kernel, n_rows=n_rows),
        grid=(b, pairs),
        in_specs=[spec(dk2), spec(dk2), spec(dv2), spec(dk2), spec(dk2)],
        out_specs=spec(dv2),
        out_shape=jax.ShapeDtypeStruct((b, s, dv_all), F32),
        scratch_shapes=[pltpu.VMEM((dk2, dv2), F32), pltpu.VMEM((dk2, dv2), F32)],
        compiler_params=pltpu.CompilerParams(dimension_semantics=("parallel", "parallel"),
                                             vmem_limit_bytes=VMEM_LIMIT),
        name="gla",
    )(q, k, v, laf, lab)


def _post_kernel(x_ref, z1r_ref, z1i_ref, og_ref, g_ref, gt_ref, onw_ref, wo_f_ref, wo_g_ref,
                 npost_ref, nffn_ref, wgt_ref, wup_ref, wdn_ref, nffn_post_ref, o_ref, ys, *, dv, ff_chunk):
    n2, tb, d = x_ref.shape
    tm = n2 * tb
    flat = lambda ref: ref[...].reshape(tm, ref.shape[2])
    groups = ys.shape[0]
    for kl in range(tb):
        rows = slice(kl * n2, (kl + 1) * n2)
        rhs = jnp.concatenate([z1r_ref[rows, :], z1i_ref[rows, :]], axis=0).astype(BF16)
        y = _dot(gt_ref[kl], rhs)
        for s in range(groups):
            ys[s, pl.ds(kl, n2, stride=tb), :] = y[:, s * LANES:(s + 1) * LANES]
    yf = jnp.concatenate([ys[s] for s in range(groups)], axis=1).astype(BF16)

    x = flat(x_ref)
    o = flat(og_ref)
    g = flat(g_ref)
    onw = onw_ref[...]
    yg = []
    for hd in range(o.shape[1] // dv):
        sl = slice(hd * dv, (hd + 1) * dv)
        gh = g[:, sl]
        yg.append((_rms(o[:, sl], onw) * (gh * jax.nn.sigmoid(gh))).astype(BF16))
    yg = jnp.concatenate(yg, axis=1)
    m = _dot(yf, wo_f_ref[...]) + _dot(yg, wo_g_ref[...])
    x = x + _rms(m, npost_ref[...])

    h2 = _rms(x, nffn_ref[...]).astype(BF16)
    d_ff = wgt_ref.shape[1]
    f = None
    for c0 in range(0, d_ff, ff_chunk):
        sl = slice(c0, c0 + ff_chunk)
        gt = _dot(h2, wgt_ref[:, sl])
        up = _dot(h2, wup_ref[:, sl])
        act = (gt * jax.nn.sigmoid(gt) * up).astype(BF16)
        part = _dot(act, wdn_ref[sl, :])
        f = part if f is None else f + part
    o_ref[...] = (x + _rms(f, nffn_post_ref[...])).reshape(n2, tb, d)


def _post(x4, z1r, z1i, og, g, gtab, onw, w_out, npost, nffn, w_gate, w_up, w_down, nffn_post, *, ff_chunk):
    batch, n1, _, d = x4.shape
    d_f = z1r.shape[3]
    d_v = og.shape[3]
    dv = d_v // GLA_HEADS
    nb = n1 // TB
    wo = w_out.astype(BF16)
    vec = lambda a: a[None, :]
    consts = [vec(onw), wo[:d_f], wo[d_f:], vec(npost), vec(nffn), w_gate.astype(BF16), w_up.astype(BF16),
              w_down.astype(BF16), vec(nffn_post)]
    z_spec = pl.BlockSpec((None, TB * n1, d_f), lambda i: (i // nb, i % nb, 0))
    z3 = lambda z: z.reshape(batch, n1 * n1, d_f)
    return pl.pallas_call(
        functools.partial(_post_kernel, dv=dv, ff_chunk=ff_chunk),
        grid=(batch * nb,),
        in_specs=[_tile_spec(n1, d), z_spec, z_spec, _tile_spec(n1, d_v), _tile_spec(n1, d_v),
                  pl.BlockSpec((TB, n1, 2 * n1), lambda i: (i % nb, 0, 0))]
                 + [_const_spec(a) for a in consts],
        out_specs=_tile_spec(n1, d),
        out_shape=jax.ShapeDtypeStruct(x4.shape, F32),
        scratch_shapes=[pltpu.VMEM((F_GROUPS, n1 * TB, LANES), F32)],
        compiler_params=pltpu.CompilerParams(dimension_semantics=("parallel",),
                                             vmem_limit_bytes=VMEM_LIMIT),
        name="post",
    )(x4, z3(z1r), z3(z1i), og, g, gtab, *consts)


def kernel(x, norm_mix_pre, w_in, w_alpha_fwd, b_alpha_fwd, w_alpha_bwd, b_alpha_bwd, gla_out_norm,
           w_out, norm_mix_post, norm_ffn_pre, w_ffn_gate, w_ffn_up, w_ffn_down, norm_ffn_post):
    batch, seq, d = x.shape
    depth = w_in.shape[0]
    fc, m1, gtab = _dft_constants(seq, d // 2 // F_GROUPS)
    n1 = m1.shape[0] // 2
    x4 = x.reshape(batch, n1, n1, d)
    seq3 = lambda a: a.reshape(batch, seq, a.shape[3])
    tile4 = lambda a: a.reshape(batch, n1, n1, a.shape[2])
    for l in range(depth):
        z1r, z1i, q, k, v, g, laf, lab = _in_proj(
            x4, norm_mix_pre[l], w_in[l], w_alpha_fwd[l], b_alpha_fwd[l], w_alpha_bwd[l], b_alpha_bwd[l],
            fc, m1)
        og = _gla(seq3(q), seq3(k), seq3(v), seq3(laf), seq3(lab), n_rows=1024)
        x4 = _post(x4, z1r, z1i, tile4(og), g, gtab, gla_out_norm[l], w_out[l], norm_mix_post[l],
                   norm_ffn_pre[l], w_ffn_gate[l], w_ffn_up[l], w_ffn_down[l], norm_ffn_post[l],
                   ff_chunk=256)
    return x4.reshape(batch, seq, d)
```

```python
import functools

import numpy as np
import jax
import jax.numpy as jnp
from jax import lax
from jax.experimental import pallas as pl
from jax.experimental.pallas import tpu as pltpu

F_GROUPS = 4
GLA_HEADS = 4
GATE_RANK = 16
GATE_LOGIT_NORMALIZER = 16.0
CHUNK = 64
EPS = 1e-6

LANES = 128
SUBLANES = 8
TB_POST = SUBLANES
TB_IN = 2 * SUBLANES
VMEM_LIMIT = 56 * 1024 * 1024

BF16 = jnp.bfloat16
F32 = jnp.float32


def _rms(x, w):
    return x * lax.rsqrt(jnp.mean(x * x, axis=-1, keepdims=True) + EPS) * w


def _dot(a, b):
    return jnp.dot(a, b, preferred_element_type=F32)


def _const_spec(a):
    return pl.BlockSpec(a.shape, lambda *_: (0,) * a.ndim, pipeline_mode=pl.Buffered(1))


def _tile_spec(n1, tb, width):
    nb = n1 // tb
    return pl.BlockSpec((None, n1, tb, width), lambda i: (i // nb, 0, i % nb, 0))


def _dft_constants(seq, group_dim):
    n1 = int(round(seq ** 0.5))
    assert n1 * n1 == seq
    c = np.arange(group_dim)
    ang = 2.0 * np.pi * ((c[:, None] * c[None, :]) % group_dim) / group_dim
    fc = np.concatenate([np.cos(ang), -np.sin(ang)], axis=1) / np.sqrt(group_dim)
    a = np.arange(n1)
    ang1 = 2.0 * np.pi * ((a[:, None] * a[None, :]) % n1) / n1
    c1, s1 = np.cos(ang1), np.sin(ang1)
    m1 = np.block([[c1, s1], [-s1, c1]]) / n1
    k1 = np.arange(n1)[:, None, None]
    k2 = np.arange(n1)[None, :, None]
    b = np.arange(n1)[None, None, :]
    ang2 = 2.0 * np.pi * ((b * (k1 + n1 * k2)) % seq) / seq
    g = np.concatenate([np.cos(ang2), np.sin(ang2)], axis=2)
    return (jnp.asarray(fc, F32).astype(BF16), jnp.asarray(m1, F32).astype(BF16),
            jnp.asarray(g, F32).astype(BF16))


def _in_proj_kernel(x_ref, nw_ref, wf_ref, wq_ref, wk_ref, wv_ref, wg_ref, wab_ref, fc_ref,
                    wgate_ref, bgate_ref, m1_ref,
                    z1r_ref, z1i_ref, q_ref, k_ref, v_ref, g_ref, laf_ref, lab_ref,
                    zs_r, zs_i, os_r, os_i, qkv_scr):
    n1, tb, d = x_ref.shape
    n_sub, groups = zs_r.shape[:2]
    d_k = laf_ref.shape[2]
    sb = tb // n_sub
    rows = n1 * sb
    tiled = lambda val: val.reshape(n1, sb, val.shape[1])
    for i in range(n_sub):
        bsl = slice(i * sb, (i + 1) * sb)
        h = _rms(x_ref[:, bsl, :].reshape(rows, d), nw_ref[...]).astype(BF16)
        ab = _dot(h, wab_ref[...]).astype(BF16)
        logits = _dot(ab, wgate_ref[...]) + bgate_ref[...]
        la = (jnp.minimum(logits, 0.0) - jnp.log1p(jnp.exp(-jnp.abs(logits)))) * (1.0 / GATE_LOGIT_NORMALIZER)
        laf_ref[:, bsl, :] = tiled(la[:, :d_k])
        lab_ref[:, bsl, :] = tiled(la[:, d_k:])
        fp = _dot(h, wf_ref[...]).astype(BF16)
        for grp in range(groups):
            zz = _dot(fp[:, grp * LANES:(grp + 1) * LANES], fc_ref[...])
            zs_r[i, grp] = zz[:, :LANES]
            zs_i[i, grp] = zz[:, LANES:]
        qkv_scr[:, bsl, :] = tiled(jnp.concatenate(
            [_dot(h, wq_ref[...]), _dot(h, wk_ref[...]), _dot(h, wv_ref[...])], axis=1))
        g_ref[:, bsl, :] = tiled(_dot(h, wg_ref[...]))
    for i in range(n_sub):
        for bl in range(sb):
            strided = pl.ds(bl, n1, stride=sb)
            zr = jnp.concatenate([zs_r[i, s, strided, :] for s in range(groups)], axis=1)
            zi = jnp.concatenate([zs_i[i, s, strided, :] for s in range(groups)], axis=1)
            out = _dot(m1_ref[...], jnp.concatenate([zr, zi], axis=0).astype(BF16))
            for s in range(groups):
                os_r[i, s, strided, :] = out[:n1, s * LANES:(s + 1) * LANES]
                os_i[i, s, strided, :] = out[n1:, s * LANES:(s + 1) * LANES]
    for s in range(groups):
        lanes = slice(s * LANES, (s + 1) * LANES)
        z1r_ref[:, :, lanes] = jnp.concatenate([tiled(os_r[i, s]) for i in range(n_sub)], axis=1).astype(BF16)
        z1i_ref[:, :, lanes] = jnp.concatenate([tiled(os_i[i, s]) for i in range(n_sub)], axis=1).astype(BF16)
    d_v = v_ref.shape[2]
    q_ref[...] = qkv_scr[:, :, :d_k].astype(BF16)
    k_ref[...] = qkv_scr[:, :, d_k:2 * d_k].astype(BF16)
    v_ref[...] = qkv_scr[:, :, 2 * d_k:2 * d_k + d_v].astype(BF16)


def _in_proj(x4, nw, w_in, w_af, b_af, w_ab, b_ab, fc, m1):
    batch, n1, _, d = x4.shape
    d_f = d // 2
    d_v = d - d_f
    d_k = d_v // 2
    assert d_f // F_GROUPS == LANES
    o = np.cumsum([0, d_f, d_k, d_k, d_v, d_v, GATE_RANK, GATE_RANK])
    wb = w_in.astype(BF16)
    wf, wq, wk, wv, wg = (wb[:, o[i]:o[i + 1]] for i in range(5))
    wab = jnp.pad(wb[:, o[5]:o[7]], ((0, 0), (0, LANES - 2 * GATE_RANK)))
    wgate = jnp.zeros((LANES, 2 * d_k), F32)
    wgate = wgate.at[:GATE_RANK, :d_k].set(w_af).at[GATE_RANK:2 * GATE_RANK, d_k:].set(w_ab).astype(BF16)
    bgate = jnp.concatenate([b_af, b_ab])[None, :]
    consts = [nw[None, :], wf, wq, wk, wv, wg, wab, fc, wgate, bgate, m1]
    outs = [(d_f, BF16), (d_f, BF16), (d_k, BF16), (d_k, BF16), (d_v, BF16), (d_v, F32), (d_k, F32), (d_k, F32)]
    n_sub = TB_IN // SUBLANES
    slab = pltpu.VMEM((n_sub, F_GROUPS, n1 * SUBLANES, LANES), F32)
    return pl.pallas_call(
        _in_proj_kernel,
        grid=(batch * (n1 // TB_IN),),
        in_specs=[_tile_spec(n1, TB_IN, d)] + [_const_spec(a) for a in consts],
        out_specs=[_tile_spec(n1, TB_IN, w) for w, _ in outs],
        out_shape=[jax.ShapeDtypeStruct((batch, n1, n1, w), dt) for w, dt in outs],
        scratch_shapes=[slab, slab, slab, slab, pltpu.VMEM((n1, TB_IN, 2 * d_k + d_v), F32)],
        compiler_params=pltpu.CompilerParams(dimension_semantics=("parallel",),
                                             vmem_limit_bytes=VMEM_LIMIT),
        name="in_proj",
    )(x4, *consts)


def _gla_constants(c, dk, dv):
    dk2, dv2 = 2 * dk, 2 * dv
    one_zero = lambda cond: jnp.where(cond, 1.0, 0.0).astype(BF16)
    ri = lax.broadcasted_iota(jnp.int32, (c, 2 * c), 0)
    cj = lax.broadcasted_iota(jnp.int32, (c, 2 * c), 1) % c
    lane_k = lax.broadcasted_iota(jnp.int32, (c, dk2), 1)
    lane_v = lax.broadcasted_iota(jnp.int32, (c, dv2), 1)
    sel_r = lax.broadcasted_iota(jnp.int32, (2 * SUBLANES, dk2), 0)
    sel_l = lax.broadcasted_iota(jnp.int32, (2 * SUBLANES, dk2), 1)
    xs_r = lax.broadcasted_iota(jnp.int32, (2 * SUBLANES, dv2), 0)
    xs_l = lax.broadcasted_iota(jnp.int32, (2 * SUBLANES, dv2), 1)
    own = (sel_r < 4) & ((sel_l < dk) == (sel_r < 2))
    xsel = ((xs_r < 2) & (xs_l >= dv)) | ((xs_r >= 2) & (xs_r < 4) & (xs_l < dv))
    return dict(lower=one_zero(cj <= ri), upper=one_zero(cj >= ri),
                k_head=(one_zero(lane_k < dk), one_zero(lane_k >= dk)),
                v_head=(one_zero(lane_v < dv), one_zero(lane_v >= dv)),
                sel_hi=own & (sel_r % 2 == 0), sel_lo=own & (sel_r % 2 == 1), xsel=one_zero(xsel))


def _gla_group(q_ref, k_ref, v_ref, o_ref, scans, cst, *, n_rows, accumulate):
    c = CHUNK
    dk = q_ref.shape[2] // 2
    dv = v_ref.shape[2] // 2
    nc = n_rows // c
    work = [(la_ref, pl.ds(pl.multiple_of(start + i * c, c), c), reverse)
            for la_ref, _, start, reverse in scans for i in range(nc)]
    idx = range(len(work))
    tri2 = [cst["upper"] if rev else cst["lower"] for _, _, rev in work]
    per_head = lambda a, masks: jnp.concatenate([a * masks[0], a * masks[1]], axis=0)

    cum = []
    for w in idx:
        la_ref, rows, _ = work[w]
        la = la_ref[0, rows, :]
        la_hi = la.astype(BF16)
        la_lo = (la - la_hi.astype(F32)).astype(BF16)
        cum.append(_dot(tri2[w], jnp.concatenate([la_hi, la_lo], axis=0)))
    tot = [cum[w][0:1, :] if work[w][2] else cum[w][c - 1:c, :] for w in idx]
    dec = [jnp.exp(t) for t in tot]
    qe = [(q_ref[0, work[w][1], :].astype(F32) * (jnp.exp(cum[w]) * (dk ** -0.5))).astype(BF16) for w in idx]
    ke = [k_ref[0, work[w][1], :].astype(F32) * jnp.exp(-cum[w]) for w in idx]
    att = [lax.dot_general(qe[w], per_head(ke[w].astype(BF16), cst["k_head"]), (((1,), (1,)), ((), ())),
                           preferred_element_type=F32) for w in idx]
    att = [att[w].astype(BF16) * tri2[w] for w in idx]
    v_bd = [per_head(v_ref[0, work[w][1], :], cst["v_head"]) for w in idx]
    o_intra = [_dot(att[w], v_bd[w]) for w in idx]

    m = []
    for w in idx:
        dec_hi = dec[w].astype(BF16).astype(F32)
        dsel = jnp.where(cst["sel_hi"], dec_hi, jnp.where(cst["sel_lo"], dec[w] - dec_hi, 0.0)).astype(BF16)
        kend_bd = per_head((ke[w] * dec[w]).astype(BF16), cst["k_head"])
        m.append(lax.dot_general(jnp.concatenate([kend_bd, dsel], axis=0),
                                 jnp.concatenate([v_bd[w], cst["xsel"]], axis=0),
                                 (((0,), (0,)), ((), ())), preferred_element_type=F32))

    zero = jnp.zeros((dk, dv), BF16)
    for n, (_, s_ref, _, reverse) in enumerate(scans):
        s0, s1 = s_ref[0], s_ref[1]
        for i in (reversed(range(nc)) if reverse else range(nc)):
            w = n * nc + i
            rows = work[w][1]
            s_bd = jnp.concatenate([jnp.concatenate([s0.astype(BF16), zero], axis=1),
                                    jnp.concatenate([zero, s1.astype(BF16)], axis=1)], axis=0)
            o = o_intra[w] + _dot(qe[w], s_bd)
            o_ref[0, rows, :] = o_ref[0, rows, :] + o if accumulate else o
            s0 = s0 * m[w][:dk, dv:] + m[w][:dk, :dv]
            s1 = s1 * m[w][dk:, :dv] + m[w][dk:, dv:]
        s_ref[0] = s0
        s_ref[1] = s1


def _gla_kernel(q_ref, k_ref, v_ref, lf_ref, lb_ref, o_ref, sf_ref, sb_ref, *, n_rows, group_rows):
    sf_ref[...] = jnp.zeros_like(sf_ref)
    sb_ref[...] = jnp.zeros_like(sb_ref)
    cst = _gla_constants(CHUNK, q_ref.shape[2] // 2, v_ref.shape[2] // 2)
    seq = q_ref.shape[1]
    nblk = seq // n_rows

    def sweep(accumulate):
        def body(j, carry):
            for r0 in range(0, n_rows, group_rows):
                fwd_start = j * n_rows + r0
                scans = [(lf_ref, sf_ref, fwd_start, False),
                         (lb_ref, sb_ref, seq - group_rows - fwd_start, True)]
                _gla_group(q_ref, k_ref, v_ref, o_ref, scans, cst, n_rows=group_rows, accumulate=accumulate)
            return carry
        return body

    lax.fori_loop(0, nblk // 2, sweep(False), 0)
    lax.fori_loop(nblk // 2, nblk, sweep(True), 0)


def _gla(q, k, v, laf, lab, *, n_rows, group_rows):
    b, s, dk_all = q.shape
    dv_all = v.shape[2]
    pairs = GLA_HEADS // 2
    dk2, dv2 = dk_all // pairs, dv_all // pairs
    assert s % (2 * n_rows) == 0 and n_rows % group_rows == 0
    spec = lambda w: pl.BlockSpec((1, s, w), lambda bi, p: (bi, 0, p))
    return pl.pallas_call(
        functools.partial(_gla_kernel, n_rows=n_rows, group_rows=group_rows),
        grid=(b, pairs),
        in_specs=[spec(dk2), spec(dk2), spec(dv2), spec(dk2), spec(dk2)],
        out_specs=spec(dv2),
        out_shape=jax.ShapeDtypeStruct((b, s, dv_all), F32),
        scratch_shapes=[pltpu.VMEM((2, dk2 // 2, dv2 // 2), F32), pltpu.VMEM((2, dk2 // 2, dv2 // 2), F32)],
        compiler_params=pltpu.CompilerParams(dimension_semantics=("parallel", "parallel"),
                                             vmem_limit_bytes=VMEM_LIMIT),
        name="gla",
    )(q, k, v, laf, lab)


def _post_kernel(x_ref, z1r_ref, z1i_ref, og_ref, g_ref, gt_ref, onw_ref, wo_f_ref, wo_g_ref,
                 npost_ref, nffn_ref, wgt_ref, wup_ref, wdn_ref, nffn_post_ref, o_ref, ys, *, dv, ff_chunk):
    n2, tb, d = x_ref.shape
    tm = n2 * tb
    flat = lambda ref: ref[...].reshape(tm, ref.shape[2])
    groups = ys.shape[0]
    for kl in range(tb):
        rows = slice(kl * n2, (kl + 1) * n2)
        rhs = jnp.concatenate([z1r_ref[rows, :], z1i_ref[rows, :]], axis=0)
        y = _dot(gt_ref[kl], rhs)
        for s in range(groups):
            ys[s, pl.ds(kl, n2, stride=tb), :] = y[:, s * LANES:(s + 1) * LANES]
    yf = jnp.concatenate([ys[s] for s in range(groups)], axis=1).astype(BF16)

    x = flat(x_ref)
    o = flat(og_ref)
    g = flat(g_ref)
    onw = onw_ref[...]
    yg = []
    for hd in range(o.shape[1] // dv):
        sl = slice(hd * dv, (hd + 1) * dv)
        gh = g[:, sl]
        yg.append((_rms(o[:, sl], onw) * (gh * jax.nn.sigmoid(gh))).astype(BF16))
    yg = jnp.concatenate(yg, axis=1)
    m = _dot(yf, wo_f_ref[...]) + _dot(yg, wo_g_ref[...])
    x = x + _rms(m, npost_ref[...])

    h2 = _rms(x, nffn_ref[...]).astype(BF16)
    d_ff = wgt_ref.shape[1]
    f = None
    for c0 in range(0, d_ff, ff_chunk):
        sl = slice(c0, c0 + ff_chunk)
        gt = _dot(h2, wgt_ref[:, sl])
        up = _dot(h2, wup_ref[:, sl])
        act = (gt * jax.nn.sigmoid(gt) * up).astype(BF16)
        part = _dot(act, wdn_ref[sl, :])
        f = part if f is None else f + part
    o_ref[...] = (x + _rms(f, nffn_post_ref[...])).reshape(n2, tb, d)


def _post(x4, z1r, z1i, og, g, gtab, onw, w_out, npost, nffn, w_gate, w_up, w_down, nffn_post, *, ff_chunk):
    batch, n1, _, d = x4.shape
    d_f = z1r.shape[3]
    d_v = og.shape[3]
    dv = d_v // GLA_HEADS
    nb = n1 // TB_POST
    wo = w_out.astype(BF16)
    vec = lambda a: a[None, :]
    consts = [vec(onw), wo[:d_f], wo[d_f:], vec(npost), vec(nffn), w_gate.astype(BF16), w_up.astype(BF16),
              w_down.astype(BF16), vec(nffn_post)]
    z_spec = pl.BlockSpec((None, TB_POST * n1, d_f), lambda i: (i // nb, i % nb, 0))
    z3 = lambda z: z.reshape(batch, n1 * n1, d_f)
    return pl.pallas_call(
        functools.partial(_post_kernel, dv=dv, ff_chunk=ff_chunk),
        grid=(batch * nb,),
        in_specs=[_tile_spec(n1, TB_POST, d), z_spec, z_spec, _tile_spec(n1, TB_POST, d_v),
                  _tile_spec(n1, TB_POST, d_v), pl.BlockSpec((TB_POST, n1, 2 * n1), lambda i: (i % nb, 0, 0))]
                 + [_const_spec(a) for a in consts],
        out_specs=_tile_spec(n1, TB_POST, d),
        out_shape=jax.ShapeDtypeStruct(x4.shape, F32),
        scratch_shapes=[pltpu.VMEM((F_GROUPS, n1 * TB_POST, LANES), F32)],
        compiler_params=pltpu.CompilerParams(dimension_semantics=("parallel",),
                                             vmem_limit_bytes=VMEM_LIMIT),
        name="post",
    )(x4, z3(z1r), z3(z1i), og, g, gtab, *consts)


def kernel(x, norm_mix_pre, w_in, w_alpha_fwd, b_alpha_fwd, w_alpha_bwd, b_alpha_bwd, gla_out_norm,
           w_out, norm_mix_post, norm_ffn_pre, w_ffn_gate, w_ffn_up, w_ffn_down, norm_ffn_post):
    batch, seq, d = x.shape
    depth = w_in.shape[0]
    fc, m1, gtab = _dft_constants(seq, d // 2 // F_GROUPS)
    n1 = m1.shape[0] // 2
    x4 = x.reshape(batch, n1, n1, d)
    seq3 = lambda a: a.reshape(batch, seq, a.shape[3])
    tile4 = lambda a: a.reshape(batch, n1, n1, a.shape[2])
    for l in range(depth):
        z1r, z1i, q, k, v, g, laf, lab = _in_proj(
            x4, norm_mix_pre[l], w_in[l], w_alpha_fwd[l], b_alpha_fwd[l], w_alpha_bwd[l], b_alpha_bwd[l],
            fc, m1)
        og = _gla(seq3(q), seq3(k), seq3(v), seq3(laf), seq3(lab), n_rows=1024, group_rows=512)
        x4 = _post(x4, z1r, z1i, tile4(og), g, gtab, gla_out_norm[l], w_out[l], norm_mix_post[l],
                   norm_ffn_pre[l], w_ffn_gate[l], w_ffn_up[l], w_ffn_down[l], norm_ffn_post[l],
                   ff_chunk=256)
    return x4.reshape(batch, seq, d)
```

```python
import functools

import numpy as np
import jax
import jax.numpy as jnp
from jax import lax
from jax.experimental import pallas as pl
from jax.experimental.pallas import tpu as pltpu

F_GROUPS = 4
GLA_HEADS = 4
GATE_RANK = 16
GATE_LOGIT_NORMALIZER = 16.0
CHUNK = 64
EPS = 1e-6

LANES = 128
SUBLANES = 8
TB_POST = SUBLANES
TB_IN = 2 * SUBLANES
VMEM_LIMIT = 56 * 1024 * 1024

BF16 = jnp.bfloat16
F32 = jnp.float32


def _rms(x, w):
    return x * lax.rsqrt(jnp.mean(x * x, axis=-1, keepdims=True) + EPS) * w


def _dot(a, b):
    return jnp.dot(a, b, preferred_element_type=F32)


def _const_spec(a):
    return pl.BlockSpec(a.shape, lambda *_: (0,) * a.ndim, pipeline_mode=pl.Buffered(1))


def _tile_spec(n1, tb, width):
    nb = n1 // tb
    return pl.BlockSpec((None, n1, tb, width), lambda i: (i // nb, 0, i % nb, 0))


def _dft_constants(seq, group_dim):
    n1 = int(round(seq ** 0.5))
    assert n1 * n1 == seq
    c = np.arange(group_dim)
    ang = 2.0 * np.pi * ((c[:, None] * c[None, :]) % group_dim) / group_dim
    fc = np.concatenate([np.cos(ang), -np.sin(ang)], axis=1) / np.sqrt(group_dim)
    a = np.arange(n1)
    ang1 = 2.0 * np.pi * ((a[:, None] * a[None, :]) % n1) / n1
    c1, s1 = np.cos(ang1), np.sin(ang1)
    m1 = np.block([[c1, s1], [-s1, c1]]) / n1
    k1 = np.arange(n1)[:, None, None]
    k2 = np.arange(n1)[None, :, None]
    b = np.arange(n1)[None, None, :]
    ang2 = 2.0 * np.pi * ((b * (k1 + n1 * k2)) % seq) / seq
    g = np.concatenate([np.cos(ang2), np.sin(ang2)], axis=2)
    return (jnp.asarray(fc, F32).astype(BF16), jnp.asarray(m1, F32).astype(BF16),
            jnp.asarray(g, F32).astype(BF16))


def _in_proj_kernel(x_ref, nw_ref, wf_ref, wq_ref, wk_ref, wv_ref, wg_ref, wab_ref, fc_ref,
                    wgate_ref, bgate_ref, m1_ref,
                    z1r_ref, z1i_ref, q_ref, k_ref, v_ref, g_ref, laf_ref, lab_ref,
                    zs_r, zs_i, os_r, os_i, qkv_scr):
    n1, tb, d = x_ref.shape
    n_sub, groups = zs_r.shape[:2]
    d_k = laf_ref.shape[2]
    sb = tb // n_sub
    rows = n1 * sb
    tiled = lambda val: val.reshape(n1, sb, val.shape[1])
    for i in range(n_sub):
        bsl = slice(i * sb, (i + 1) * sb)
        h = _rms(x_ref[:, bsl, :].reshape(rows, d), nw_ref[...]).astype(BF16)
        ab = _dot(h, wab_ref[...]).astype(BF16)
        logits = _dot(ab, wgate_ref[...]) + bgate_ref[...]
        la = (jnp.minimum(logits, 0.0) - jnp.log1p(jnp.exp(-jnp.abs(logits)))) * (1.0 / GATE_LOGIT_NORMALIZER)
        laf_ref[:, bsl, :] = tiled(la[:, :d_k])
        lab_ref[:, bsl, :] = tiled(la[:, d_k:])
        fp = _dot(h, wf_ref[...]).astype(BF16)
        for grp in range(groups):
            zz = _dot(fp[:, grp * LANES:(grp + 1) * LANES], fc_ref[...])
            zs_r[i, grp] = zz[:, :LANES]
            zs_i[i, grp] = zz[:, LANES:]
        qkv_scr[:, bsl, :] = tiled(jnp.concatenate(
            [_dot(h, wq_ref[...]), _dot(h, wk_ref[...]), _dot(h, wv_ref[...])], axis=1))
        g_ref[:, bsl, :] = tiled(_dot(h, wg_ref[...]))
    for i in range(n_sub):
        for bl in range(sb):
            strided = pl.ds(bl, n1, stride=sb)
            zr = jnp.concatenate([zs_r[i, s, strided, :] for s in range(groups)], axis=1)
            zi = jnp.concatenate([zs_i[i, s, strided, :] for s in range(groups)], axis=1)
            out = _dot(m1_ref[...], jnp.concatenate([zr, zi], axis=0).astype(BF16))
            for s in range(groups):
                os_r[i, s, strided, :] = out[:n1, s * LANES:(s + 1) * LANES]
                os_i[i, s, strided, :] = out[n1:, s * LANES:(s + 1) * LANES]
    for s in range(groups):
        lanes = slice(s * LANES, (s + 1) * LANES)
        z1r_ref[:, :, lanes] = jnp.concatenate([tiled(os_r[i, s]) for i in range(n_sub)], axis=1).astype(BF16)
        z1i_ref[:, :, lanes] = jnp.concatenate([tiled(os_i[i, s]) for i in range(n_sub)], axis=1).astype(BF16)
    d_v = v_ref.shape[2]
    q_ref[...] = qkv_scr[:, :, :d_k].astype(BF16)
    k_ref[...] = qkv_scr[:, :, d_k:2 * d_k].astype(BF16)
    v_ref[...] = qkv_scr[:, :, 2 * d_k:2 * d_k + d_v].astype(BF16)


def _in_proj(x4, nw, w_in, w_af, b_af, w_ab, b_ab, fc, m1):
    batch, n1, _, d = x4.shape
    d_f = d // 2
    d_v = d - d_f
    d_k = d_v // 2
    assert d_f // F_GROUPS == LANES
    o = np.cumsum([0, d_f, d_k, d_k, d_v, d_v, GATE_RANK, GATE_RANK])
    wb = w_in.astype(BF16)
    wf, wq, wk, wv, wg = (wb[:, o[i]:o[i + 1]] for i in range(5))
    wab = jnp.pad(wb[:, o[5]:o[7]], ((0, 0), (0, LANES - 2 * GATE_RANK)))
    wgate = jnp.zeros((LANES, 2 * d_k), F32)
    wgate = wgate.at[:GATE_RANK, :d_k].set(w_af).at[GATE_RANK:2 * GATE_RANK, d_k:].set(w_ab).astype(BF16)
    bgate = jnp.concatenate([b_af, b_ab])[None, :]
    consts = [nw[None, :], wf, wq, wk, wv, wg, wab, fc, wgate, bgate, m1]
    outs = [(d_f, BF16), (d_f, BF16), (d_k, BF16), (d_k, BF16), (d_v, BF16), (d_v, F32), (d_k, F32), (d_k, F32)]
    n_sub = TB_IN // SUBLANES
    slab = pltpu.VMEM((n_sub, F_GROUPS, n1 * SUBLANES, LANES), F32)
    return pl.pallas_call(
        _in_proj_kernel,
        grid=(batch * (n1 // TB_IN),),
        in_specs=[_tile_spec(n1, TB_IN, d)] + [_const_spec(a) for a in consts],
        out_specs=[_tile_spec(n1, TB_IN, w) for w, _ in outs],
        out_shape=[jax.ShapeDtypeStruct((batch, n1, n1, w), dt) for w, dt in outs],
        scratch_shapes=[slab, slab, slab, slab, pltpu.VMEM((n1, TB_IN, 2 * d_k + d_v), F32)],
        compiler_params=pltpu.CompilerParams(dimension_semantics=("parallel",),
                                             vmem_limit_bytes=VMEM_LIMIT),
        name="in_proj",
    )(x4, *consts)


def _gla_constants(c, dk, dv):
    dk2, dv2 = 2 * dk, 2 * dv
    one_zero = lambda cond: jnp.where(cond, 1.0, 0.0).astype(BF16)
    ri = lax.broadcasted_iota(jnp.int32, (c, 2 * c), 0)
    cj = lax.broadcasted_iota(jnp.int32, (c, 2 * c), 1) % c
    lane_k = lax.broadcasted_iota(jnp.int32, (c, dk2), 1)
    lane_v = lax.broadcasted_iota(jnp.int32, (c, dv2), 1)
    sel_r = lax.broadcasted_iota(jnp.int32, (2 * SUBLANES, dk2), 0)
    sel_l = lax.broadcasted_iota(jnp.int32, (2 * SUBLANES, dk2), 1)
    xs_r = lax.broadcasted_iota(jnp.int32, (2 * SUBLANES, dv2), 0)
    xs_l = lax.broadcasted_iota(jnp.int32, (2 * SUBLANES, dv2), 1)
    own = (sel_r < 4) & ((sel_l < dk) == (sel_r < 2))
    xsel = ((xs_r < 2) & (xs_l >= dv)) | ((xs_r >= 2) & (xs_r < 4) & (xs_l < dv))
    return dict(lower=one_zero(cj <= ri), upper=one_zero(cj >= ri),
                k_head=(one_zero(lane_k < dk), one_zero(lane_k >= dk)),
                v_head=(one_zero(lane_v < dv), one_zero(lane_v >= dv)),
                sel_hi=own & (sel_r % 2 == 0), sel_lo=own & (sel_r % 2 == 1), xsel=one_zero(xsel))


def _gla_group(q_ref, k_ref, v_ref, o_ref, scans, cst, *, n_rows, accumulate):
    c = CHUNK
    dk = q_ref.shape[2] // 2
    dv = v_ref.shape[2] // 2
    nc = n_rows // c
    work = [(la_ref, pl.ds(pl.multiple_of(start + i * c, c), c), reverse)
            for la_ref, _, start, reverse in scans for i in range(nc)]
    idx = range(len(work))
    tri2 = [cst["upper"] if rev else cst["lower"] for _, _, rev in work]
    per_head = lambda a, masks: jnp.concatenate([a * masks[0], a * masks[1]], axis=0)

    cum = []
    for w in idx:
        la_ref, rows, _ = work[w]
        la = la_ref[0, rows, :]
        la_hi = la.astype(BF16)
        la_lo = (la - la_hi.astype(F32)).astype(BF16)
        cum.append(_dot(tri2[w], jnp.concatenate([la_hi, la_lo], axis=0)))
    tot = [cum[w][0:1, :] if work[w][2] else cum[w][c - 1:c, :] for w in idx]
    dec = [jnp.exp(t) for t in tot]
    qe = [(q_ref[0, work[w][1], :].astype(F32) * (jnp.exp(cum[w]) * (dk ** -0.5))).astype(BF16) for w in idx]
    ke = [k_ref[0, work[w][1], :].astype(F32) * jnp.exp(-cum[w]) for w in idx]
    att = [lax.dot_general(qe[w], per_head(ke[w].astype(BF16), cst["k_head"]), (((1,), (1,)), ((), ())),
                           preferred_element_type=F32) for w in idx]
    att = [att[w].astype(BF16) * tri2[w] for w in idx]
    v_bd = [per_head(v_ref[0, work[w][1], :], cst["v_head"]) for w in idx]
    o_intra = [_dot(att[w], v_bd[w]) for w in idx]

    m = []
    for w in idx:
        dec_hi = dec[w].astype(BF16).astype(F32)
        dsel = jnp.where(cst["sel_hi"], dec_hi, jnp.where(cst["sel_lo"], dec[w] - dec_hi, 0.0)).astype(BF16)
        kend_bd = per_head((ke[w] * dec[w]).astype(BF16), cst["k_head"])
        m.append(lax.dot_general(jnp.concatenate([kend_bd, dsel], axis=0),
                                 jnp.concatenate([v_bd[w], cst["xsel"]], axis=0),
                                 (((0,), (0,)), ((), ())), preferred_element_type=F32))

    zero = jnp.zeros((dk, dv), BF16)
    for n, (_, s_ref, _, reverse) in enumerate(scans):
        s0, s1 = s_ref[0], s_ref[1]
        for i in (reversed(range(nc)) if reverse else range(nc)):
            w = n * nc + i
            rows = work[w][1]
            s_bd = jnp.concatenate([jnp.concatenate([s0.astype(BF16), zero], axis=1),
                                    jnp.concatenate([zero, s1.astype(BF16)], axis=1)], axis=0)
            o = o_intra[w] + _dot(qe[w], s_bd)
            o_ref[0, rows, :] = o_ref[0, rows, :] + o if accumulate else o
            s0 = s0 * m[w][:dk, dv:] + m[w][:dk, :dv]
            s1 = s1 * m[w][dk:, :dv] + m[w][dk:, dv:]
        s_ref[0] = s0
        s_ref[1] = s1


def _gla_kernel(q_ref, k_ref, v_ref, lf_ref, lb_ref, o_ref, sf_ref, sb_ref, *, n_rows, group_rows):
    sf_ref[...] = jnp.zeros_like(sf_ref)
    sb_ref[...] = jnp.zeros_like(sb_ref)
    cst = _gla_constants(CHUNK, q_ref.shape[2] // 2, v_ref.shape[2] // 2)
    seq = q_ref.shape[1]
    nblk = seq // n_rows

    def sweep(accumulate):
        def body(j, carry):
            for r0 in range(0, n_rows, group_rows):
                fwd_start = j * n_rows + r0
                scans = [(lf_ref, sf_ref, fwd_start, False),
                         (lb_ref, sb_ref, seq - group_rows - fwd_start, True)]
                _gla_group(q_ref, k_ref, v_ref, o_ref, scans, cst, n_rows=group_rows, accumulate=accumulate)
            return carry
        return body

    lax.fori_loop(0, nblk // 2, sweep(False), 0)
    lax.fori_loop(nblk // 2, nblk, sweep(True), 0)


def _gla(q, k, v, laf, lab, *, n_rows, group_rows):
    b, s, dk_all = q.shape
    dv_all = v.shape[2]
    pairs = GLA_HEADS // 2
    dk2, dv2 = dk_all // pairs, dv_all // pairs
    assert s % (2 * n_rows) == 0 and n_rows % group_rows == 0
    spec = lambda w: pl.BlockSpec((1, s, w), lambda bi, p: (bi, 0, p))
    return pl.pallas_call(
        functools.partial(_gla_kernel, n_rows=n_rows, group_rows=group_rows),
        grid=(b, pairs),
        in_specs=[spec(dk2), spec(dk2), spec(dv2), spec(dk2), spec(dk2)],
        out_specs=spec(dv2),
        out_shape=jax.ShapeDtypeStruct((b, s, dv_all), F32),
        scratch_shapes=[pltpu.VMEM((2, dk2 // 2, dv2 // 2), F32), pltpu.VMEM((2, dk2 // 2, dv2 // 2), F32)],
        compiler_params=pltpu.CompilerParams(dimension_semantics=("parallel", "parallel"),
                                             vmem_limit_bytes=VMEM_LIMIT),
        name="gla",
    )(q, k, v, laf, lab)


def _post_kernel(x_ref, z1r_ref, z1i_ref, og_ref, g_ref, gt_ref, onw_ref, wo_f_ref, wo_g_ref,
                 npost_ref, nffn_ref, wgt_ref, wup_ref, wdn_ref, nffn_post_ref, o_ref, ys, *, dv, ff_chunk):
    n2, tb, d = x_ref.shape
    groups = ys.shape[0]
    for kl in range(tb):
        rows = slice(kl * n2, (kl + 1) * n2)
        rhs = jnp.concatenate([z1r_ref[rows, :], z1i_ref[rows, :]], axis=0)
        y = _dot(gt_ref[kl], rhs)
        for s in range(groups):
            ys[s, pl.ds(kl, n2, stride=tb), :] = y[:, s * LANES:(s + 1) * LANES]

    halves = 2
    hn2 = n2 // halves
    hrows = hn2 * tb
    flat = lambda ref, i: ref[i * hn2:(i + 1) * hn2].reshape(hrows, ref.shape[2])
    onw = onw_ref[...]
    xs, h2s = [], []
    for i in range(halves):
        rsl = slice(i * hrows, (i + 1) * hrows)
        yf = jnp.concatenate([ys[s, rsl, :] for s in range(groups)], axis=1).astype(BF16)
        o = flat(og_ref, i)
        g = flat(g_ref, i)
        yg = []
        for hd in range(o.shape[1] // dv):
            sl = slice(hd * dv, (hd + 1) * dv)
            gh = g[:, sl]
            yg.append((_rms(o[:, sl], onw) * (gh * jax.nn.sigmoid(gh))).astype(BF16))
        m = _dot(yf, wo_f_ref[...]) + _dot(jnp.concatenate(yg, axis=1), wo_g_ref[...])
        xs.append(flat(x_ref, i) + _rms(m, npost_ref[...]))
        h2s.append(_rms(xs[i], nffn_ref[...]).astype(BF16))

    def ffn_chunk(h2, c0):
        sl = slice(c0, c0 + ff_chunk)
        gt = _dot(h2, wgt_ref[:, sl])
        up = _dot(h2, wup_ref[:, sl])
        act = (gt * jax.nn.sigmoid(gt) * up).astype(BF16)
        return _dot(act, wdn_ref[sl, :])

    d_ff = wgt_ref.shape[1]
    starts = list(range(0, d_ff, ff_chunk))
    first = [ffn_chunk(h2s[i], starts[0]) for i in range(halves)]
    h2 = jnp.concatenate(h2s, axis=0)
    mid = None
    for c0 in starts[1:-1]:
        part = ffn_chunk(h2, c0)
        mid = part if mid is None else mid + part
    for i in range(halves):
        f = first[i] + mid[i * hrows:(i + 1) * hrows] + ffn_chunk(h2s[i], starts[-1])
        o_ref[i * hn2:(i + 1) * hn2] = (xs[i] + _rms(f, nffn_post_ref[...])).reshape(hn2, tb, d)


def _post(x4, z1r, z1i, og, g, gtab, onw, w_out, npost, nffn, w_gate, w_up, w_down, nffn_post, *, ff_chunk):
    batch, n1, _, d = x4.shape
    d_f = z1r.shape[3]
    d_v = og.shape[3]
    dv = d_v // GLA_HEADS
    nb = n1 // TB_POST
    wo = w_out.astype(BF16)
    vec = lambda a: a[None, :]
    consts = [vec(onw), wo[:d_f], wo[d_f:], vec(npost), vec(nffn), w_gate.astype(BF16), w_up.astype(BF16),
              w_down.astype(BF16), vec(nffn_post)]
    z_spec = pl.BlockSpec((None, TB_POST * n1, d_f), lambda i: (i // nb, i % nb, 0))
    z3 = lambda z: z.reshape(batch, n1 * n1, d_f)
    return pl.pallas_call(
        functools.partial(_post_kernel, dv=dv, ff_chunk=ff_chunk),
        grid=(batch * nb,),
        in_specs=[_tile_spec(n1, TB_POST, d), z_spec, z_spec, _tile_spec(n1, TB_POST, d_v),
                  _tile_spec(n1, TB_POST, d_v), pl.BlockSpec((TB_POST, n1, 2 * n1), lambda i: (i % nb, 0, 0))]
                 + [_const_spec(a) for a in consts],
        out_specs=_tile_spec(n1, TB_POST, d),
        out_shape=jax.ShapeDtypeStruct(x4.shape, F32),
        scratch_shapes=[pltpu.VMEM((F_GROUPS, n1 * TB_POST, LANES), F32)],
        compiler_params=pltpu.CompilerParams(dimension_semantics=("parallel",),
                                             vmem_limit_bytes=VMEM_LIMIT),
        name="post",
    )(x4, z3(z1r), z3(z1i), og, g, gtab, *consts)


def kernel(x, norm_mix_pre, w_in, w_alpha_fwd, b_alpha_fwd, w_alpha_bwd, b_alpha_bwd, gla_out_norm,
           w_out, norm_mix_post, norm_ffn_pre, w_ffn_gate, w_ffn_up, w_ffn_down, norm_ffn_post):
    batch, seq, d = x.shape
    depth = w_in.shape[0]
    fc, m1, gtab = _dft_constants(seq, d // 2 // F_GROUPS)
    n1 = m1.shape[0] // 2
    x4 = x.reshape(batch, n1, n1, d)
    seq3 = lambda a: a.reshape(batch, seq, a.shape[3])
    tile4 = lambda a: a.reshape(batch, n1, n1, a.shape[2])
    for l in range(depth):
        z1r, z1i, q, k, v, g, laf, lab = _in_proj(
            x4, norm_mix_pre[l], w_in[l], w_alpha_fwd[l], b_alpha_fwd[l], w_alpha_bwd[l], b_alpha_bwd[l],
            fc, m1)
        og = _gla(seq3(q), seq3(k), seq3(v), seq3(laf), seq3(lab), n_rows=1024, group_rows=512)
        x4 = _post(x4, z1r, z1i, tile4(og), g, gtab, gla_out_norm[l], w_out[l], norm_mix_post[l],
                   norm_ffn_pre[l], w_ffn_gate[l], w_ffn_up[l], w_ffn_down[l], norm_ffn_post[l],
                   ff_chunk=256)
    return x4.reshape(batch, seq, d)
```

```python
import functools

import numpy as np
import jax
import jax.numpy as jnp
from jax import lax
from jax.experimental import pallas as pl
from jax.experimental.pallas import tpu as pltpu

F_GROUPS = 4
GLA_HEADS = 4
GATE_RANK = 16
GATE_LOGIT_NORMALIZER = 16.0
CHUNK = 64
EPS = 1e-6

LANES = 128
SUBLANES = 8
TB_POST = SUBLANES
TB_IN = 2 * SUBLANES
VMEM_LIMIT = 56 * 1024 * 1024

BF16 = jnp.bfloat16
F32 = jnp.float32


def _rms(x, w):
    return x * lax.rsqrt(jnp.mean(x * x, axis=-1, keepdims=True) + EPS) * w


def _dot(a, b):
    return jnp.dot(a, b, preferred_element_type=F32)


def _const_spec(a):
    return pl.BlockSpec(a.shape, lambda *_: (0,) * a.ndim, pipeline_mode=pl.Buffered(1))


def _layer_spec(a, layer):
    return pl.BlockSpec((None,) + a.shape[1:], lambda *_: (layer,) + (0,) * (a.ndim - 1),
                        pipeline_mode=pl.Buffered(1))


def _tile_spec(n1, tb, width):
    nb = n1 // tb
    return pl.BlockSpec((None, n1, tb, width), lambda i: (i // nb, 0, i % nb, 0))


def _dft_constants(seq, group_dim):
    n1 = int(round(seq ** 0.5))
    assert n1 * n1 == seq
    c = np.arange(group_dim)
    ang = 2.0 * np.pi * ((c[:, None] * c[None, :]) % group_dim) / group_dim
    fc = np.concatenate([np.cos(ang), -np.sin(ang)], axis=1) / np.sqrt(group_dim)
    a = np.arange(n1)
    ang1 = 2.0 * np.pi * ((a[:, None] * a[None, :]) % n1) / n1
    c1, s1 = np.cos(ang1), np.sin(ang1)
    m1 = np.block([[c1, s1], [-s1, c1]]) / n1
    k1 = np.arange(n1)[:, None, None]
    k2 = np.arange(n1)[None, :, None]
    b = np.arange(n1)[None, None, :]
    ang2 = 2.0 * np.pi * ((b * (k1 + n1 * k2)) % seq) / seq
    g = np.concatenate([np.cos(ang2), np.sin(ang2)], axis=2)
    return (jnp.asarray(fc, F32).astype(BF16), jnp.asarray(m1, F32).astype(BF16),
            jnp.asarray(g, F32).astype(BF16))


def _in_proj_kernel(x_ref, nw_ref, w_ref, wgate_ref, bgate_ref, fc_ref, m1_ref,
                    z1r_ref, z1i_ref, q_ref, k_ref, v_ref, g_ref, laf_ref, lab_ref,
                    zs_r, zs_i, os_r, os_i, qkv_scr):
    n1, tb, d = x_ref.shape
    n_sub, groups = zs_r.shape[:2]
    d_k = laf_ref.shape[2]
    d_v = v_ref.shape[2]
    d_f = z1r_ref.shape[2]
    wf_ref = w_ref.at[:, :d_f]
    wqkv_ref = w_ref.at[:, d_f:d_f + 2 * d_k + d_v]
    wg_ref = w_ref.at[:, d_f + 2 * d_k + d_v:d_f + 2 * d_k + 2 * d_v]
    wab_ref = w_ref.at[:, d_f + 2 * d_k + 2 * d_v:]
    sb = tb // n_sub
    rows = n1 * sb
    tiled = lambda val: val.reshape(n1, sb, val.shape[1])
    for i in range(n_sub):
        bsl = slice(i * sb, (i + 1) * sb)
        h = _rms(x_ref[:, bsl, :].reshape(rows, d), nw_ref[...]).astype(BF16)
        ab = _dot(h, wab_ref[...]).astype(BF16)
        logits = _dot(ab, wgate_ref[...]) + bgate_ref[...]
        fp = _dot(h, wf_ref[...]).astype(BF16)
        for grp in range(groups):
            zz = _dot(fp[:, grp * LANES:(grp + 1) * LANES], fc_ref[...])
            zs_r[i, grp] = zz[:, :LANES]
            zs_i[i, grp] = zz[:, LANES:]
        qkv_scr[:, bsl, :] = tiled(_dot(h, wqkv_ref[...]))
        g_ref[:, bsl, :] = tiled(_dot(h, wg_ref[...]))
        la = (jnp.minimum(logits, 0.0) - jnp.log(1.0 + jnp.exp(-jnp.abs(logits)))) * (1.0 / GATE_LOGIT_NORMALIZER)
        laf_ref[:, bsl, :] = tiled(la[:, :d_k])
        lab_ref[:, bsl, :] = tiled(la[:, d_k:])
    for i in range(n_sub):
        for bl in range(sb):
            strided = pl.ds(bl, n1, stride=sb)
            zr = jnp.concatenate([zs_r[i, s, strided, :] for s in range(groups)], axis=1)
            zi = jnp.concatenate([zs_i[i, s, strided, :] for s in range(groups)], axis=1)
            out = _dot(m1_ref[...], jnp.concatenate([zr, zi], axis=0).astype(BF16))
            for s in range(groups):
                os_r[i, s, strided, :] = out[:n1, s * LANES:(s + 1) * LANES]
                os_i[i, s, strided, :] = out[n1:, s * LANES:(s + 1) * LANES]
    for s in range(groups):
        lanes = slice(s * LANES, (s + 1) * LANES)
        z1r_ref[:, :, lanes] = jnp.concatenate([tiled(os_r[i, s]) for i in range(n_sub)], axis=1).astype(BF16)
        z1i_ref[:, :, lanes] = jnp.concatenate([tiled(os_i[i, s]) for i in range(n_sub)], axis=1).astype(BF16)
    q_ref[...] = qkv_scr[:, :, :d_k].astype(BF16)
    k_ref[...] = qkv_scr[:, :, d_k:2 * d_k].astype(BF16)
    v_ref[...] = qkv_scr[:, :, 2 * d_k:2 * d_k + d_v].astype(BF16)


def _in_proj_params(nw, w_in, w_af, b_af, w_ab, b_ab):
    d_k = w_af.shape[2]
    w = jnp.pad(w_in, ((0, 0), (0, 0), (0, LANES - 2 * GATE_RANK))).astype(BF16)
    wgate = jnp.concatenate([jnp.pad(w_af, ((0, 0), (0, 0), (0, d_k))), jnp.pad(w_ab, ((0, 0), (0, 0), (d_k, 0)))],
                            axis=1)
    wgate = jnp.pad(wgate, ((0, 0), (0, LANES - 2 * GATE_RANK), (0, 0))).astype(BF16)
    bgate = jnp.concatenate([b_af, b_ab], axis=1)[:, None, :]
    return [nw[:, None, :], w, wgate, bgate]


def _in_proj(x4, layer, params, fc, m1):
    batch, n1, _, d = x4.shape
    d_f = d // 2
    d_v = d - d_f
    d_k = d_v // 2
    assert d_f // F_GROUPS == LANES
    assert params[1].shape[2] == d_f + 2 * d_k + 2 * d_v + LANES
    consts = [fc, m1]
    outs = [(d_f, BF16), (d_f, BF16), (d_k, BF16), (d_k, BF16), (d_v, BF16), (d_v, F32), (d_k, F32), (d_k, F32)]
    n_sub = TB_IN // SUBLANES
    slab = pltpu.VMEM((n_sub, F_GROUPS, n1 * SUBLANES, LANES), F32)
    return pl.pallas_call(
        _in_proj_kernel,
        grid=(batch * (n1 // TB_IN),),
        in_specs=[_tile_spec(n1, TB_IN, d)] + [_layer_spec(a, layer) for a in params]
                 + [_const_spec(a) for a in consts],
        out_specs=[_tile_spec(n1, TB_IN, w) for w, _ in outs],
        out_shape=[jax.ShapeDtypeStruct((batch, n1, n1, w), dt) for w, dt in outs],
        scratch_shapes=[slab, slab, slab, slab, pltpu.VMEM((n1, TB_IN, 2 * d_k + d_v), F32)],
        compiler_params=pltpu.CompilerParams(dimension_semantics=("parallel",),
                                             vmem_limit_bytes=VMEM_LIMIT),
        name="in_proj",
    )(x4, *params, *consts)


def _gla_constants(c, dk, dv):
    dk2, dv2 = 2 * dk, 2 * dv
    one_zero = lambda cond: jnp.where(cond, 1.0, 0.0).astype(BF16)
    ri = lax.broadcasted_iota(jnp.int32, (c, 2 * c), 0)
    cj = lax.broadcasted_iota(jnp.int32, (c, 2 * c), 1) % c
    lane_k = lax.broadcasted_iota(jnp.int32, (c, dk2), 1)
    lane_v = lax.broadcasted_iota(jnp.int32, (c, dv2), 1)
    sel_r = lax.broadcasted_iota(jnp.int32, (2 * SUBLANES, dk2), 0)
    sel_l = lax.broadcasted_iota(jnp.int32, (2 * SUBLANES, dk2), 1)
    xs_r = lax.broadcasted_iota(jnp.int32, (2 * SUBLANES, dv2), 0)
    xs_l = lax.broadcasted_iota(jnp.int32, (2 * SUBLANES, dv2), 1)
    own = (sel_r < 4) & ((sel_l < dk) == (sel_r < 2))
    xsel = ((xs_r < 2) & (xs_l >= dv)) | ((xs_r >= 2) & (xs_r < 4) & (xs_l < dv))
    return dict(lower=one_zero(cj <= ri), upper=one_zero(cj >= ri),
                k_head=(one_zero(lane_k < dk), one_zero(lane_k >= dk)),
                v_head=(one_zero(lane_v < dv), one_zero(lane_v >= dv)),
                sel_hi=own & (sel_r % 2 == 0), sel_lo=own & (sel_r % 2 == 1), xsel=one_zero(xsel))


def _gla_group(q_ref, k_ref, v_ref, o_ref, scans, cst, *, n_rows, accumulate):
    c = CHUNK
    dk = q_ref.shape[2] // 2
    dv = v_ref.shape[2] // 2
    nc = n_rows // c
    work = [(la_ref, pl.ds(pl.multiple_of(start + i * c, c), c), reverse)
            for la_ref, _, start, reverse in scans for i in range(nc)]
    idx = range(len(work))
    tri2 = [cst["upper"] if rev else cst["lower"] for _, _, rev in work]
    per_head = lambda a, masks: jnp.concatenate([a * masks[0], a * masks[1]], axis=0)

    cum = []
    for w in idx:
        la_ref, rows, _ = work[w]
        la = la_ref[0, rows, :]
        la_hi = la.astype(BF16)
        la_lo = (la - la_hi.astype(F32)).astype(BF16)
        cum.append(_dot(tri2[w], jnp.concatenate([la_hi, la_lo], axis=0)))
    tot = [cum[w][0:1, :] if work[w][2] else cum[w][c - 1:c, :] for w in idx]
    dec = [jnp.exp(t) for t in tot]
    qe = [(q_ref[0, work[w][1], :].astype(F32) * (jnp.exp(cum[w]) * (dk ** -0.5))).astype(BF16) for w in idx]
    ke = [k_ref[0, work[w][1], :].astype(F32) * jnp.exp(-cum[w]) for w in idx]
    att = [lax.dot_general(qe[w], per_head(ke[w].astype(BF16), cst["k_head"]), (((1,), (1,)), ((), ())),
                           preferred_element_type=F32) for w in idx]
    att = [att[w].astype(BF16) * tri2[w] for w in idx]
    v_bd = [per_head(v_ref[0, work[w][1], :], cst["v_head"]) for w in idx]
    o_intra = [_dot(att[w], v_bd[w]) for w in idx]

    m = []
    for w in idx:
        dec_hi = dec[w].astype(BF16).astype(F32)
        dsel = jnp.where(cst["sel_hi"], dec_hi, jnp.where(cst["sel_lo"], dec[w] - dec_hi, 0.0)).astype(BF16)
        kend_bd = per_head((ke[w] * dec[w]).astype(BF16), cst["k_head"])
        m.append(lax.dot_general(jnp.concatenate([kend_bd, dsel], axis=0),
                                 jnp.concatenate([v_bd[w], cst["xsel"]], axis=0),
                                 (((0,), (0,)), ((), ())), preferred_element_type=F32))

    zero = jnp.zeros((dk, dv), BF16)
    for n, (_, s_ref, _, reverse) in enumerate(scans):
        s0, s1 = s_ref[0], s_ref[1]
        for i in (reversed(range(nc)) if reverse else range(nc)):
            w = n * nc + i
            rows = work[w][1]
            s_bd = jnp.concatenate([jnp.concatenate([s0.astype(BF16), zero], axis=1),
                                    jnp.concatenate([zero, s1.astype(BF16)], axis=1)], axis=0)
            o = o_intra[w] + _dot(qe[w], s_bd)
            o_ref[0, rows, :] = o_ref[0, rows, :] + o if accumulate else o
            s0 = s0 * m[w][:dk, dv:] + m[w][:dk, :dv]
            s1 = s1 * m[w][dk:, :dv] + m[w][dk:, dv:]
        s_ref[0] = s0
        s_ref[1] = s1


def _gla_kernel(q_ref, k_ref, v_ref, lf_ref, lb_ref, o_ref, sf_ref, sb_ref, *, n_rows, group_rows):
    sf_ref[...] = jnp.zeros_like(sf_ref)
    sb_ref[...] = jnp.zeros_like(sb_ref)
    cst = _gla_constants(CHUNK, q_ref.shape[2] // 2, v_ref.shape[2] // 2)
    seq = q_ref.shape[1]
    nblk = seq // n_rows

    def sweep(accumulate):
        def body(j, carry):
            for r0 in range(0, n_rows, group_rows):
                fwd_start = j * n_rows + r0
                scans = [(lf_ref, sf_ref, fwd_start, False),
                         (lb_ref, sb_ref, seq - group_rows - fwd_start, True)]
                _gla_group(q_ref, k_ref, v_ref, o_ref, scans, cst, n_rows=group_rows, accumulate=accumulate)
            return carry
        return body

    lax.fori_loop(0, nblk // 2, sweep(False), 0)
    lax.fori_loop(nblk // 2, nblk, sweep(True), 0)


def _gla(q, k, v, laf, lab, *, n_rows, group_rows):
    b, s, dk_all = q.shape
    dv_all = v.shape[2]
    pairs = GLA_HEADS // 2
    dk2, dv2 = dk_all // pairs, dv_all // pairs
    assert s % (2 * n_rows) == 0 and n_rows % group_rows == 0
    spec = lambda w: pl.BlockSpec((1, s, w), lambda bi, p: (bi, 0, p))
    return pl.pallas_call(
        functools.partial(_gla_kernel, n_rows=n_rows, group_rows=group_rows),
        grid=(b, pairs),
        in_specs=[spec(dk2), spec(dk2), spec(dv2), spec(dk2), spec(dk2)],
        out_specs=spec(dv2),
        out_shape=jax.ShapeDtypeStruct((b, s, dv_all), F32),
        scratch_shapes=[pltpu.VMEM((2, dk2 // 2, dv2 // 2), F32), pltpu.VMEM((2, dk2 // 2, dv2 // 2), F32)],
        compiler_params=pltpu.CompilerParams(dimension_semantics=("parallel", "parallel"),
                                             vmem_limit_bytes=VMEM_LIMIT),
        name="gla",
    )(q, k, v, laf, lab)


def _post_kernel(x_ref, z1r_ref, z1i_ref, og_ref, g_ref, gt_ref, onw_ref, wo_ref,
                 npost_ref, nffn_ref, wgt_ref, wup_ref, wdn_ref, nffn_post_ref, o_ref, ys, *, dv, ff_chunk):
    n2, tb, d = x_ref.shape
    groups = ys.shape[0]
    d_f = z1r_ref.shape[1]
    wo_f_ref, wo_g_ref = wo_ref.at[:d_f], wo_ref.at[d_f:]
    for kl in range(tb):
        rows = slice(kl * n2, (kl + 1) * n2)
        rhs = jnp.concatenate([z1r_ref[rows, :], z1i_ref[rows, :]], axis=0)
        y = _dot(gt_ref[kl], rhs)
        for s in range(groups):
            ys[s, pl.ds(kl, n2, stride=tb), :] = y[:, s * LANES:(s + 1) * LANES]

    halves = 2
    hn2 = n2 // halves
    hrows = hn2 * tb
    flat = lambda ref, i: ref[i * hn2:(i + 1) * hn2].reshape(hrows, ref.shape[2])
    onw = onw_ref[...]
    xs, h2s = [], []
    for i in range(halves):
        rsl = slice(i * hrows, (i + 1) * hrows)
        yf = jnp.concatenate([ys[s, rsl, :] for s in range(groups)], axis=1).astype(BF16)
        o = flat(og_ref, i)
        g = flat(g_ref, i)
        yg = []
        for hd in range(o.shape[1] // dv):
            sl = slice(hd * dv, (hd + 1) * dv)
            gh = g[:, sl]
            yg.append((_rms(o[:, sl], onw) * (gh * jax.nn.sigmoid(gh))).astype(BF16))
        m = _dot(yf, wo_f_ref[...]) + _dot(jnp.concatenate(yg, axis=1), wo_g_ref[...])
        xs.append(flat(x_ref, i) + _rms(m, npost_ref[...]))
        h2s.append(_rms(xs[i], nffn_ref[...]).astype(BF16))

    def ffn_chunk(h2, c0):
        sl = slice(c0, c0 + ff_chunk)
        gt = _dot(h2, wgt_ref[:, sl])
        up = _dot(h2, wup_ref[:, sl])
        act = (gt * jax.nn.sigmoid(gt) * up).astype(BF16)
        return _dot(act, wdn_ref[sl, :])

    d_ff = wgt_ref.shape[1]
    starts = list(range(0, d_ff, ff_chunk))
    first = [ffn_chunk(h2s[i], starts[0]) for i in range(halves)]
    h2 = jnp.concatenate(h2s, axis=0)
    mid = None
    for c0 in starts[1:-1]:
        part = ffn_chunk(h2, c0)
        mid = part if mid is None else mid + part
    for i in range(halves):
        f = first[i] + mid[i * hrows:(i + 1) * hrows] + ffn_chunk(h2s[i], starts[-1])
        o_ref[i * hn2:(i + 1) * hn2] = (xs[i] + _rms(f, nffn_post_ref[...])).reshape(hn2, tb, d)


def _post_params(onw, w_out, npost, nffn, w_gate, w_up, w_down, nffn_post):
    vec = lambda a: a[:, None, :]
    return [vec(onw), w_out.astype(BF16), vec(npost), vec(nffn), w_gate.astype(BF16), w_up.astype(BF16),
            w_down.astype(BF16), vec(nffn_post)]


def _post(x4, z1r, z1i, og, g, gtab, layer, params, *, ff_chunk):
    batch, n1, _, d = x4.shape
    d_f = z1r.shape[3]
    d_v = og.shape[3]
    dv = d_v // GLA_HEADS
    nb = n1 // TB_POST
    z_spec = pl.BlockSpec((None, TB_POST * n1, d_f), lambda i: (i // nb, i % nb, 0))
    z3 = lambda z: z.reshape(batch, n1 * n1, d_f)
    return pl.pallas_call(
        functools.partial(_post_kernel, dv=dv, ff_chunk=ff_chunk),
        grid=(batch * nb,),
        in_specs=[_tile_spec(n1, TB_POST, d), z_spec, z_spec, _tile_spec(n1, TB_POST, d_v),
                  _tile_spec(n1, TB_POST, d_v), pl.BlockSpec((TB_POST, n1, 2 * n1), lambda i: (i % nb, 0, 0))]
                 + [_layer_spec(a, layer) for a in params],
        out_specs=_tile_spec(n1, TB_POST, d),
        out_shape=jax.ShapeDtypeStruct(x4.shape, F32),
        scratch_shapes=[pltpu.VMEM((F_GROUPS, n1 * TB_POST, LANES), F32)],
        compiler_params=pltpu.CompilerParams(dimension_semantics=("parallel",),
                                             vmem_limit_bytes=VMEM_LIMIT),
        name="post",
    )(x4, z3(z1r), z3(z1i), og, g, gtab, *params)


def kernel(x, norm_mix_pre, w_in, w_alpha_fwd, b_alpha_fwd, w_alpha_bwd, b_alpha_bwd, gla_out_norm,
           w_out, norm_mix_post, norm_ffn_pre, w_ffn_gate, w_ffn_up, w_ffn_down, norm_ffn_post):
    batch, seq, d = x.shape
    depth = w_in.shape[0]
    fc, m1, gtab = _dft_constants(seq, d // 2 // F_GROUPS)
    n1 = m1.shape[0] // 2
    x4 = x.reshape(batch, n1, n1, d)
    seq3 = lambda a: a.reshape(batch, seq, a.shape[3])
    tile4 = lambda a: a.reshape(batch, n1, n1, a.shape[2])
    in_params = _in_proj_params(norm_mix_pre, w_in, w_alpha_fwd, b_alpha_fwd, w_alpha_bwd, b_alpha_bwd)
    post_params = _post_params(gla_out_norm, w_out, norm_mix_post, norm_ffn_pre, w_ffn_gate, w_ffn_up,
                               w_ffn_down, norm_ffn_post)
    for l in range(depth):
        z1r, z1i, q, k, v, g, laf, lab = _in_proj(x4, l, in_params, fc, m1)
        og = _gla(seq3(q), seq3(k), seq3(v), seq3(laf), seq3(lab), n_rows=1024, group_rows=512)
        x4 = _post(x4, z1r, z1i, tile4(og), g, gtab, l, post_params, ff_chunk=256)
    return x4.reshape(batch, seq, d)
```

```python
import functools

import numpy as np
import jax
import jax.numpy as jnp
from jax import lax
from jax.experimental import pallas as pl
from jax.experimental.pallas import tpu as pltpu

F_GROUPS = 4
GLA_HEADS = 4
GATE_RANK = 16
GATE_LOGIT_NORMALIZER = 16.0
CHUNK = 64
EPS = 1e-6

LANES = 128
SUBLANES = 8
TB_POST = SUBLANES
TB_IN = 2 * SUBLANES
VMEM_LIMIT = 56 * 1024 * 1024

BF16 = jnp.bfloat16
F32 = jnp.float32


def _rms(x, w):
    return x * lax.rsqrt(jnp.mean(x * x, axis=-1, keepdims=True) + EPS) * w


def _dot(a, b):
    return jnp.dot(a, b, preferred_element_type=F32)


def _const_spec(a):
    return pl.BlockSpec(a.shape, lambda *_: (0,) * a.ndim, pipeline_mode=pl.Buffered(1))


def _layer_spec(a, layer):
    return pl.BlockSpec((None,) + a.shape[1:], lambda *_: (layer,) + (0,) * (a.ndim - 1),
                        pipeline_mode=pl.Buffered(1))


def _tile_spec(n1, tb, width):
    nb = n1 // tb
    return pl.BlockSpec((None, n1, tb, width), lambda i: (i // nb, 0, i % nb, 0))


def _dft_constants(seq, group_dim):
    n1 = int(round(seq ** 0.5))
    assert n1 * n1 == seq
    c = np.arange(group_dim)
    ang = 2.0 * np.pi * ((c[:, None] * c[None, :]) % group_dim) / group_dim
    fc = np.concatenate([np.cos(ang), -np.sin(ang)], axis=1) / np.sqrt(group_dim)
    a = np.arange(n1)
    ang1 = 2.0 * np.pi * ((a[:, None] * a[None, :]) % n1) / n1
    c1, s1 = np.cos(ang1), np.sin(ang1)
    m1 = np.block([[c1, s1], [-s1, c1]]) / n1
    k1 = np.arange(n1)[:, None, None]
    k2 = np.arange(n1)[None, :, None]
    b = np.arange(n1)[None, None, :]
    ang2 = 2.0 * np.pi * ((b * (k1 + n1 * k2)) % seq) / seq
    g = np.concatenate([np.cos(ang2), np.sin(ang2)], axis=2)
    return (jnp.asarray(fc, F32).astype(BF16), jnp.asarray(m1, F32).astype(BF16),
            jnp.asarray(g, F32).astype(BF16))


def _in_proj_kernel(x_ref, nw_ref, w_ref, wgate_ref, bgate_ref, fc_ref, m1_ref,
                    z1r_ref, z1i_ref, q_ref, k_ref, v_ref, g_ref, laf_ref, lab_ref,
                    zs_r, zs_i, os_r, os_i, qkv_scr):
    n1, tb, d = x_ref.shape
    n_sub, groups = zs_r.shape[:2]
    d_k = laf_ref.shape[2]
    d_v = v_ref.shape[2]
    d_f = z1r_ref.shape[2]
    c_qkv, c_g, c_ab = d_f, d_f + 2 * d_k + d_v, d_f + 2 * d_k + 2 * d_v
    sb = tb // n_sub
    rows = n1 * sb
    tiled = lambda val: val.reshape(n1, sb, val.shape[1])
    for i in range(n_sub):
        bsl = slice(i * sb, (i + 1) * sb)
        h = _rms(x_ref[:, bsl, :].reshape(rows, d), nw_ref[...]).astype(BF16)
        p = _dot(h, w_ref[...])
        ab = p[:, c_ab:].astype(BF16)
        logits = _dot(ab, wgate_ref[...]) + bgate_ref[...]
        fp = p[:, :c_qkv].astype(BF16)
        for grp in range(groups):
            zz = _dot(fp[:, grp * LANES:(grp + 1) * LANES], fc_ref[...])
            zs_r[i, grp] = zz[:, :LANES]
            zs_i[i, grp] = zz[:, LANES:]
        qkv_scr[:, bsl, :] = tiled(p[:, c_qkv:c_g])
        g_ref[:, bsl, :] = tiled(p[:, c_g:c_ab])
        la = (jnp.minimum(logits, 0.0) - jnp.log(1.0 + jnp.exp(-jnp.abs(logits)))) * (1.0 / GATE_LOGIT_NORMALIZER)
        laf_ref[:, bsl, :] = tiled(la[:, :d_k])
        lab_ref[:, bsl, :] = tiled(la[:, d_k:])
    for i in range(n_sub):
        for bl in range(sb):
            strided = pl.ds(bl, n1, stride=sb)
            zr = jnp.concatenate([zs_r[i, s, strided, :] for s in range(groups)], axis=1)
            zi = jnp.concatenate([zs_i[i, s, strided, :] for s in range(groups)], axis=1)
            out = _dot(m1_ref[...], jnp.concatenate([zr, zi], axis=0).astype(BF16))
            for s in range(groups):
                os_r[i, s, strided, :] = out[:n1, s * LANES:(s + 1) * LANES]
                os_i[i, s, strided, :] = out[n1:, s * LANES:(s + 1) * LANES]
    for s in range(groups):
        lanes = slice(s * LANES, (s + 1) * LANES)
        z1r_ref[:, :, lanes] = jnp.concatenate([tiled(os_r[i, s]) for i in range(n_sub)], axis=1).astype(BF16)
        z1i_ref[:, :, lanes] = jnp.concatenate([tiled(os_i[i, s]) for i in range(n_sub)], axis=1).astype(BF16)
    q_ref[...] = qkv_scr[:, :, :d_k].astype(BF16)
    k_ref[...] = qkv_scr[:, :, d_k:2 * d_k].astype(BF16)
    v_ref[...] = qkv_scr[:, :, 2 * d_k:2 * d_k + d_v].astype(BF16)


def _in_proj_params(nw, w_in, w_af, b_af, w_ab, b_ab):
    d_k = w_af.shape[2]
    w = jnp.pad(w_in, ((0, 0), (0, 0), (0, LANES - 2 * GATE_RANK))).astype(BF16)
    wgate = jnp.concatenate([jnp.pad(w_af, ((0, 0), (0, 0), (0, d_k))), jnp.pad(w_ab, ((0, 0), (0, 0), (d_k, 0)))],
                            axis=1)
    wgate = jnp.pad(wgate, ((0, 0), (0, LANES - 2 * GATE_RANK), (0, 0))).astype(BF16)
    bgate = jnp.concatenate([b_af, b_ab], axis=1)[:, None, :]
    return [nw[:, None, :], w, wgate, bgate]


def _in_proj(x4, layer, params, fc, m1):
    batch, n1, _, d = x4.shape
    d_f = d // 2
    d_v = d - d_f
    d_k = d_v // 2
    assert d_f // F_GROUPS == LANES
    assert params[1].shape[2] == d_f + 2 * d_k + 2 * d_v + LANES
    consts = [fc, m1]
    outs = [(d_f, BF16), (d_f, BF16), (d_k, BF16), (d_k, BF16), (d_v, BF16), (d_v, F32), (d_k, F32), (d_k, F32)]
    n_sub = TB_IN // SUBLANES
    slab = pltpu.VMEM((n_sub, F_GROUPS, n1 * SUBLANES, LANES), F32)
    return pl.pallas_call(
        _in_proj_kernel,
        grid=(batch * (n1 // TB_IN),),
        in_specs=[_tile_spec(n1, TB_IN, d)] + [_layer_spec(a, layer) for a in params]
                 + [_const_spec(a) for a in consts],
        out_specs=[_tile_spec(n1, TB_IN, w) for w, _ in outs],
        out_shape=[jax.ShapeDtypeStruct((batch, n1, n1, w), dt) for w, dt in outs],
        scratch_shapes=[slab, slab, slab, slab, pltpu.VMEM((n1, TB_IN, 2 * d_k + d_v), F32)],
        compiler_params=pltpu.CompilerParams(dimension_semantics=("parallel",),
                                             vmem_limit_bytes=VMEM_LIMIT),
        name="in_proj",
    )(x4, *params, *consts)


def _gla_constants(c, dk, dv):
    dk2, dv2 = 2 * dk, 2 * dv
    one_zero = lambda cond: jnp.where(cond, 1.0, 0.0).astype(BF16)
    ri = lax.broadcasted_iota(jnp.int32, (c, 2 * c), 0)
    cj = lax.broadcasted_iota(jnp.int32, (c, 2 * c), 1) % c
    lane_k = lax.broadcasted_iota(jnp.int32, (c, dk2), 1)
    lane_v = lax.broadcasted_iota(jnp.int32, (c, dv2), 1)
    sel_r = lax.broadcasted_iota(jnp.int32, (2 * SUBLANES, dk2), 0)
    sel_l = lax.broadcasted_iota(jnp.int32, (2 * SUBLANES, dk2), 1)
    xs_r = lax.broadcasted_iota(jnp.int32, (2 * SUBLANES, dv2), 0)
    xs_l = lax.broadcasted_iota(jnp.int32, (2 * SUBLANES, dv2), 1)
    own = (sel_r < 4) & ((sel_l < dk) == (sel_r < 2))
    xsel = ((xs_r < 2) & (xs_l >= dv)) | ((xs_r >= 2) & (xs_r < 4) & (xs_l < dv))
    return dict(lower=one_zero(cj <= ri), upper=one_zero(cj >= ri),
                k_head=(one_zero(lane_k < dk), one_zero(lane_k >= dk)),
                v_head=(one_zero(lane_v < dv), one_zero(lane_v >= dv)),
                sel_hi=own & (sel_r % 2 == 0), sel_lo=own & (sel_r % 2 == 1), xsel=one_zero(xsel))


class _GlaGroup:
    def __init__(self, q_ref, k_ref, v_ref, o_ref, scans, cst, n_rows):
        self.q_ref, self.k_ref, self.v_ref, self.o_ref = q_ref, k_ref, v_ref, o_ref
        self.scans, self.cst = scans, cst
        self.dk = q_ref.shape[2] // 2
        self.dv = v_ref.shape[2] // 2
        self.nc = n_rows // CHUNK
        self.work = [(la_ref, pl.ds(pl.multiple_of(start + i * CHUNK, CHUNK), CHUNK), reverse)
                     for la_ref, _, start, reverse in scans for i in range(self.nc)]
        n = len(self.work)
        self.chunks = range(n)
        self.tri2 = [cst["upper"] if rev else cst["lower"] for _, _, rev in self.work]
        self.cum, self.qe, self.ke, self.dec = [None] * n, [None] * n, [None] * n, [None] * n
        self.ke_bd, self.v_bd, self.kend_sel = [None] * n, [None] * n, [None] * n
        self.att, self.o_intra, self.m = [None] * n, [None] * n, [None] * n

    @staticmethod
    def _per_head(a, masks):
        return jnp.concatenate([a * masks[0], a * masks[1]], axis=0)

    def e1(self, w):
        la_ref, rows, _ = self.work[w]
        la = la_ref[0, rows, :]
        la_hi = la.astype(BF16)
        la_lo = (la - la_hi.astype(F32)).astype(BF16)
        self.cum[w] = _dot(self.tri2[w], jnp.concatenate([la_hi, la_lo], axis=0))

    def e2(self, w):
        _, rows, reverse = self.work[w]
        cum = self.cum[w]
        tot = cum[0:1, :] if reverse else cum[CHUNK - 1:CHUNK, :]
        self.dec[w] = jnp.exp(tot)
        self.qe[w] = (self.q_ref[0, rows, :].astype(F32) * (jnp.exp(cum) * (self.dk ** -0.5))).astype(BF16)
        self.ke[w] = self.k_ref[0, rows, :].astype(F32) * jnp.exp(-cum)

    def e3(self, w):
        cst, dec = self.cst, self.dec[w]
        self.ke_bd[w] = self._per_head(self.ke[w].astype(BF16), cst["k_head"])
        self.v_bd[w] = self._per_head(self.v_ref[0, self.work[w][1], :], cst["v_head"])
        dec_hi = dec.astype(BF16).astype(F32)
        dsel = jnp.where(cst["sel_hi"], dec_hi, jnp.where(cst["sel_lo"], dec - dec_hi, 0.0)).astype(BF16)
        self.kend_sel[w] = jnp.concatenate(
            [self._per_head((self.ke[w] * dec).astype(BF16), cst["k_head"]), dsel], axis=0)

    def m1(self, w):
        self.att[w] = lax.dot_general(self.qe[w], self.ke_bd[w], (((1,), (1,)), ((), ())),
                                      preferred_element_type=F32)

    def m2(self, w):
        att = self.att[w].astype(BF16) * self.tri2[w]
        self.o_intra[w] = _dot(att, self.v_bd[w])

    def m3(self, w):
        self.m[w] = lax.dot_general(self.kend_sel[w], jnp.concatenate([self.v_bd[w], self.cst["xsel"]], axis=0),
                                    (((0,), (0,)), ((), ())), preferred_element_type=F32)

    def finish(self, accumulate):
        dk, dv, o_ref = self.dk, self.dv, self.o_ref
        zero = jnp.zeros((dk, dv), BF16)
        for n, (_, s_ref, _, reverse) in enumerate(self.scans):
            s0, s1 = s_ref[0], s_ref[1]
            for i in (reversed(range(self.nc)) if reverse else range(self.nc)):
                w = n * self.nc + i
                rows, m = self.work[w][1], self.m[w]
                s_bd = jnp.concatenate([jnp.concatenate([s0.astype(BF16), zero], axis=1),
                                        jnp.concatenate([zero, s1.astype(BF16)], axis=1)], axis=0)
                o = self.o_intra[w] + _dot(self.qe[w], s_bd)
                o_ref[0, rows, :] = o_ref[0, rows, :] + o if accumulate else o
                s0 = s0 * m[:dk, dv:] + m[:dk, :dv]
                s1 = s1 * m[dk:, :dv] + m[dk:, dv:]
            s_ref[0] = s0
            s_ref[1] = s1


def _gla_kernel(q_ref, k_ref, v_ref, lf_ref, lb_ref, o_ref, sf_ref, sb_ref, *, n_rows, group_rows):
    sf_ref[...] = jnp.zeros_like(sf_ref)
    sb_ref[...] = jnp.zeros_like(sb_ref)
    cst = _gla_constants(CHUNK, q_ref.shape[2] // 2, v_ref.shape[2] // 2)
    seq = q_ref.shape[1]
    nblk = seq // n_rows

    def sweep(accumulate):
        def body(j, carry):
            groups = []
            for r0 in range(0, n_rows, group_rows):
                fwd_start = j * n_rows + r0
                scans = [(lf_ref, sf_ref, fwd_start, False),
                         (lb_ref, sb_ref, seq - group_rows - fwd_start, True)]
                groups.append(_GlaGroup(q_ref, k_ref, v_ref, o_ref, scans, cst, group_rows))
            first = groups[0]
            for stage in (first.e1, first.e2, first.e3):
                for w in first.chunks:
                    stage(w)
            for cur, nxt in zip(groups, groups[1:] + [None]):
                for cur_stage, nxt_stage in ((cur.m1, "e1"), (cur.m2, "e2"), (cur.m3, "e3")):
                    for w in cur.chunks:
                        cur_stage(w)
                        if nxt is not None:
                            getattr(nxt, nxt_stage)(w)
                cur.finish(accumulate)
            return carry
        return body

    lax.fori_loop(0, nblk // 2, sweep(False), 0)
    lax.fori_loop(nblk // 2, nblk, sweep(True), 0)


def _gla(q, k, v, laf, lab, *, n_rows, group_rows):
    b, s, dk_all = q.shape
    dv_all = v.shape[2]
    pairs = GLA_HEADS // 2
    dk2, dv2 = dk_all // pairs, dv_all // pairs
    assert s % (2 * n_rows) == 0 and n_rows % group_rows == 0
    spec = lambda w: pl.BlockSpec((1, s, w), lambda bi, p: (bi, 0, p))
    return pl.pallas_call(
        functools.partial(_gla_kernel, n_rows=n_rows, group_rows=group_rows),
        grid=(b, pairs),
        in_specs=[spec(dk2), spec(dk2), spec(dv2), spec(dk2), spec(dk2)],
        out_specs=spec(dv2),
        out_shape=jax.ShapeDtypeStruct((b, s, dv_all), F32),
        scratch_shapes=[pltpu.VMEM((2, dk2 // 2, dv2 // 2), F32), pltpu.VMEM((2, dk2 // 2, dv2 // 2), F32)],
        compiler_params=pltpu.CompilerParams(dimension_semantics=("parallel", "parallel"),
                                             vmem_limit_bytes=VMEM_LIMIT),
        name="gla",
    )(q, k, v, laf, lab)


def _post_kernel(x_ref, z1r_ref, z1i_ref, og_ref, g_ref, gt_ref, onw_ref, wo_ref,
                 npost_ref, nffn_ref, wgt_ref, wup_ref, wdn_ref, nffn_post_ref, o_ref, ys, *, dv, ff_chunk):
    n2, tb, d = x_ref.shape
    groups = ys.shape[0]
    d_f = z1r_ref.shape[1]
    wo_f_ref, wo_g_ref = wo_ref.at[:d_f], wo_ref.at[d_f:]
    for kl in range(tb):
        rows = slice(kl * n2, (kl + 1) * n2)
        rhs = jnp.concatenate([z1r_ref[rows, :], z1i_ref[rows, :]], axis=0)
        y = _dot(gt_ref[kl], rhs)
        for s in range(groups):
            ys[s, pl.ds(kl, n2, stride=tb), :] = y[:, s * LANES:(s + 1) * LANES]

    halves = 2
    hn2 = n2 // halves
    hrows = hn2 * tb
    flat = lambda ref, i: ref[i * hn2:(i + 1) * hn2].reshape(hrows, ref.shape[2])
    onw = onw_ref[...]
    xs, h2s = [], []
    for i in range(halves):
        rsl = slice(i * hrows, (i + 1) * hrows)
        yf = jnp.concatenate([ys[s, rsl, :] for s in range(groups)], axis=1).astype(BF16)
        o = flat(og_ref, i)
        g = flat(g_ref, i)
        yg = []
        for hd in range(o.shape[1] // dv):
            sl = slice(hd * dv, (hd + 1) * dv)
            gh = g[:, sl]
            yg.append((_rms(o[:, sl], onw) * (gh * jax.nn.sigmoid(gh))).astype(BF16))
        m = _dot(yf, wo_f_ref[...]) + _dot(jnp.concatenate(yg, axis=1), wo_g_ref[...])
        xs.append(flat(x_ref, i) + _rms(m, npost_ref[...]))
        h2s.append(_rms(xs[i], nffn_ref[...]).astype(BF16))

    def ffn_chunk(h2, c0):
        sl = slice(c0, c0 + ff_chunk)
        gt = _dot(h2, wgt_ref[:, sl])
        up = _dot(h2, wup_ref[:, sl])
        act = (gt * jax.nn.sigmoid(gt) * up).astype(BF16)
        return _dot(act, wdn_ref[sl, :])

    d_ff = wgt_ref.shape[1]
    starts = list(range(0, d_ff, ff_chunk))
    first = [ffn_chunk(h2s[i], starts[0]) for i in range(halves)]
    h2 = jnp.concatenate(h2s, axis=0)
    mid = None
    for c0 in starts[1:-1]:
        part = ffn_chunk(h2, c0)
        mid = part if mid is None else mid + part
    for i in range(halves):
        f = first[i] + mid[i * hrows:(i + 1) * hrows] + ffn_chunk(h2s[i], starts[-1])
        o_ref[i * hn2:(i + 1) * hn2] = (xs[i] + _rms(f, nffn_post_ref[...])).reshape(hn2, tb, d)


def _post_params(onw, w_out, npost, nffn, w_gate, w_up, w_down, nffn_post):
    vec = lambda a: a[:, None, :]
    return [vec(onw), w_out.astype(BF16), vec(npost), vec(nffn), w_gate.astype(BF16), w_up.astype(BF16),
            w_down.astype(BF16), vec(nffn_post)]


def _post(x4, z1r, z1i, og, g, gtab, layer, params, *, ff_chunk):
    batch, n1, _, d = x4.shape
    d_f = z1r.shape[3]
    d_v = og.shape[3]
    dv = d_v // GLA_HEADS
    nb = n1 // TB_POST
    z_spec = pl.BlockSpec((None, TB_POST * n1, d_f), lambda i: (i // nb, i % nb, 0))
    z3 = lambda z: z.reshape(batch, n1 * n1, d_f)
    return pl.pallas_call(
        functools.partial(_post_kernel, dv=dv, ff_chunk=ff_chunk),
        grid=(batch * nb,),
        in_specs=[_tile_spec(n1, TB_POST, d), z_spec, z_spec, _tile_spec(n1, TB_POST, d_v),
                  _tile_spec(n1, TB_POST, d_v), pl.BlockSpec((TB_POST, n1, 2 * n1), lambda i: (i % nb, 0, 0))]
                 + [_layer_spec(a, layer) for a in params],
        out_specs=_tile_spec(n1, TB_POST, d),
        out_shape=jax.ShapeDtypeStruct(x4.shape, F32),
        scratch_shapes=[pltpu.VMEM((F_GROUPS, n1 * TB_POST, LANES), F32)],
        compiler_params=pltpu.CompilerParams(dimension_semantics=("parallel",),
                                             vmem_limit_bytes=VMEM_LIMIT),
        name="post",
    )(x4, z3(z1r), z3(z1i), og, g, gtab, *params)


def kernel(x, norm_mix_pre, w_in, w_alpha_fwd, b_alpha_fwd, w_alpha_bwd, b_alpha_bwd, gla_out_norm,
           w_out, norm_mix_post, norm_ffn_pre, w_ffn_gate, w_ffn_up, w_ffn_down, norm_ffn_post):
    batch, seq, d = x.shape
    depth = w_in.shape[0]
    fc, m1, gtab = _dft_constants(seq, d // 2 // F_GROUPS)
    n1 = m1.shape[0] // 2
    x4 = x.reshape(batch, n1, n1, d)
    seq3 = lambda a: a.reshape(batch, seq, a.shape[3])
    tile4 = lambda a: a.reshape(batch, n1, n1, a.shape[2])
    in_params = _in_proj_params(norm_mix_pre, w_in, w_alpha_fwd, b_alpha_fwd, w_alpha_bwd, b_alpha_bwd)
    post_params = _post_params(gla_out_norm, w_out, norm_mix_post, norm_ffn_pre, w_ffn_gate, w_ffn_up,
                               w_ffn_down, norm_ffn_post)
    for l in range(depth):
        z1r, z1i, q, k, v, g, laf, lab = _in_proj(x4, l, in_params, fc, m1)
        og = _gla(seq3(q), seq3(k), seq3(v), seq3(laf), seq3(lab), n_rows=2048, group_rows=512)
        x4 = _post(x4, z1r, z1i, tile4(og), g, gtab, l, post_params, ff_chunk=256)
    return x4.reshape(batch, seq, d)
```

```python
import functools

import numpy as np
import jax
import jax.numpy as jnp
from jax import lax
from jax.experimental import pallas as pl
from jax.experimental.pallas import tpu as pltpu

F_GROUPS = 4
GLA_HEADS = 4
GATE_RANK = 16
GATE_LOGIT_NORMALIZER = 16.0
CHUNK = 64
EPS = 1e-6

LANES = 128
SUBLANES = 8
TB_POST = SUBLANES
TB_IN = 2 * SUBLANES
VMEM_LIMIT = 56 * 1024 * 1024

BF16 = jnp.bfloat16
F32 = jnp.float32


def _rms(x, w):
    return x * lax.rsqrt(jnp.mean(x * x, axis=-1, keepdims=True) + EPS) * w


def _silu(x):
    return (0.5 * x) * (1.0 + jnp.tanh(0.5 * x))


def _dot(a, b):
    return jnp.dot(a, b, preferred_element_type=F32)


def _const_spec(a):
    return pl.BlockSpec(a.shape, lambda *_: (0,) * a.ndim, pipeline_mode=pl.Buffered(1))


def _layer_spec(a, layer):
    return pl.BlockSpec((None,) + a.shape[1:], lambda *_: (layer,) + (0,) * (a.ndim - 1),
                        pipeline_mode=pl.Buffered(1))


def _tile_spec(n1, tb, width):
    nb = n1 // tb
    return pl.BlockSpec((None, n1, tb, width), lambda i: (i // nb, 0, i % nb, 0))


def _dft_constants(seq, group_dim):
    n1 = int(round(seq ** 0.5))
    assert n1 * n1 == seq
    c = np.arange(group_dim)
    ang = 2.0 * np.pi * ((c[:, None] * c[None, :]) % group_dim) / group_dim
    fc = np.concatenate([np.cos(ang), -np.sin(ang)], axis=1) / np.sqrt(group_dim)
    a = np.arange(n1)
    ang1 = 2.0 * np.pi * ((a[:, None] * a[None, :]) % n1) / n1
    c1, s1 = np.cos(ang1), np.sin(ang1)
    m1 = np.block([[c1, s1], [-s1, c1]]) / n1
    k1 = np.arange(n1)[:, None, None]
    k2 = np.arange(n1)[None, :, None]
    b = np.arange(n1)[None, None, :]
    ang2 = 2.0 * np.pi * ((b * (k1 + n1 * k2)) % seq) / seq
    g = np.concatenate([np.cos(ang2), np.sin(ang2)], axis=2)
    return (jnp.asarray(fc, F32).astype(BF16), jnp.asarray(m1, F32).astype(BF16),
            jnp.asarray(g, F32).astype(BF16))


def _in_proj_kernel(x_ref, nw_ref, w_ref, wgate_ref, bgate_ref, fc_ref, m1_ref,
                    z1r_ref, z1i_ref, q_ref, k_ref, v_ref, g_ref, laf_ref, lab_ref,
                    zs_r, zs_i, os_r, os_i, qkv_scr):
    n1, tb, d = x_ref.shape
    n_sub, groups = zs_r.shape[:2]
    d_k = laf_ref.shape[2]
    d_v = v_ref.shape[2]
    d_f = z1r_ref.shape[2]
    c_qkv, c_g, c_ab = d_f, d_f + 2 * d_k + d_v, d_f + 2 * d_k + 2 * d_v
    sb = tb // n_sub
    rows = n1 * sb
    tiled = lambda val: val.reshape(n1, sb, val.shape[1])
    for i in range(n_sub):
        bsl = slice(i * sb, (i + 1) * sb)
        h = _rms(x_ref[:, bsl, :].reshape(rows, d), nw_ref[...]).astype(BF16)
        p = _dot(h, w_ref[...])
        ab = p[:, c_ab:].astype(BF16)
        logits = _dot(ab, wgate_ref[...]) + bgate_ref[...]
        fp = p[:, :c_qkv].astype(BF16)
        for grp in range(groups):
            zz = _dot(fp[:, grp * LANES:(grp + 1) * LANES], fc_ref[...])
            zs_r[i, grp] = zz[:, :LANES]
            zs_i[i, grp] = zz[:, LANES:]
        qkv_scr[:, bsl, :] = tiled(p[:, c_qkv:c_g])
        g_ref[:, bsl, :] = tiled(p[:, c_g:c_ab])
        la = (jnp.minimum(logits, 0.0) - jnp.log(1.0 + jnp.exp(-jnp.abs(logits)))) * (1.0 / GATE_LOGIT_NORMALIZER)
        laf_ref[:, bsl, :] = tiled(la[:, :d_k])
        lab_ref[:, bsl, :] = tiled(la[:, d_k:])
    for i in range(n_sub):
        for bl in range(sb):
            strided = pl.ds(bl, n1, stride=sb)
            zr = jnp.concatenate([zs_r[i, s, strided, :] for s in range(groups)], axis=1)
            zi = jnp.concatenate([zs_i[i, s, strided, :] for s in range(groups)], axis=1)
            out = _dot(m1_ref[...], jnp.concatenate([zr, zi], axis=0).astype(BF16))
            for s in range(groups):
                os_r[i, s, strided, :] = out[:n1, s * LANES:(s + 1) * LANES]
                os_i[i, s, strided, :] = out[n1:, s * LANES:(s + 1) * LANES]
    for s in range(groups):
        lanes = slice(s * LANES, (s + 1) * LANES)
        z1r_ref[:, :, lanes] = jnp.concatenate([tiled(os_r[i, s]) for i in range(n_sub)], axis=1).astype(BF16)
        z1i_ref[:, :, lanes] = jnp.concatenate([tiled(os_i[i, s]) for i in range(n_sub)], axis=1).astype(BF16)
    q_ref[...] = qkv_scr[:, :, :d_k].astype(BF16)
    k_ref[...] = qkv_scr[:, :, d_k:2 * d_k].astype(BF16)
    v_ref[...] = qkv_scr[:, :, 2 * d_k:2 * d_k + d_v].astype(BF16)


def _in_proj_params(nw, w_in, w_af, b_af, w_ab, b_ab):
    d_k = w_af.shape[2]
    w = jnp.pad(w_in, ((0, 0), (0, 0), (0, LANES - 2 * GATE_RANK))).astype(BF16)
    wgate = jnp.concatenate([jnp.pad(w_af, ((0, 0), (0, 0), (0, d_k))), jnp.pad(w_ab, ((0, 0), (0, 0), (d_k, 0)))],
                            axis=1)
    wgate = jnp.pad(wgate, ((0, 0), (0, LANES - 2 * GATE_RANK), (0, 0))).astype(BF16)
    bgate = jnp.concatenate([b_af, b_ab], axis=1)[:, None, :]
    return [nw[:, None, :], w, wgate, bgate]


def _in_proj(x4, layer, params, fc, m1):
    batch, n1, _, d = x4.shape
    d_f = d // 2
    d_v = d - d_f
    d_k = d_v // 2
    assert d_f // F_GROUPS == LANES
    assert params[1].shape[2] == d_f + 2 * d_k + 2 * d_v + LANES
    consts = [fc, m1]
    outs = [(d_f, BF16), (d_f, BF16), (d_k, BF16), (d_k, BF16), (d_v, BF16), (d_v, F32), (d_k, F32), (d_k, F32)]
    n_sub = TB_IN // SUBLANES
    slab = pltpu.VMEM((n_sub, F_GROUPS, n1 * SUBLANES, LANES), F32)
    return pl.pallas_call(
        _in_proj_kernel,
        grid=(batch * (n1 // TB_IN),),
        in_specs=[_tile_spec(n1, TB_IN, d)] + [_layer_spec(a, layer) for a in params]
                 + [_const_spec(a) for a in consts],
        out_specs=[_tile_spec(n1, TB_IN, w) for w, _ in outs],
        out_shape=[jax.ShapeDtypeStruct((batch, n1, n1, w), dt) for w, dt in outs],
        scratch_shapes=[slab, slab, slab, slab, pltpu.VMEM((n1, TB_IN, 2 * d_k + d_v), F32)],
        compiler_params=pltpu.CompilerParams(dimension_semantics=("parallel",),
                                             vmem_limit_bytes=VMEM_LIMIT),
        name="in_proj",
    )(x4, *params, *consts)


def _gla_constants(c, dk, dv):
    dk2, dv2 = 2 * dk, 2 * dv
    one_zero = lambda cond: jnp.where(cond, 1.0, 0.0).astype(BF16)
    ri = lax.broadcasted_iota(jnp.int32, (c, 2 * c), 0)
    cj = lax.broadcasted_iota(jnp.int32, (c, 2 * c), 1) % c
    lane_k = lax.broadcasted_iota(jnp.int32, (c, dk2), 1)
    lane_v = lax.broadcasted_iota(jnp.int32, (c, dv2), 1)
    sel_r = lax.broadcasted_iota(jnp.int32, (2 * SUBLANES, dk2), 0)
    sel_l = lax.broadcasted_iota(jnp.int32, (2 * SUBLANES, dk2), 1)
    xs_r = lax.broadcasted_iota(jnp.int32, (2 * SUBLANES, dv2), 0)
    xs_l = lax.broadcasted_iota(jnp.int32, (2 * SUBLANES, dv2), 1)
    own = (sel_r < 4) & ((sel_l < dk) == (sel_r < 2))
    xsel = ((xs_r < 2) & (xs_l >= dv)) | ((xs_r >= 2) & (xs_r < 4) & (xs_l < dv))
    return dict(lower=one_zero(cj <= ri), upper=one_zero(cj >= ri),
                keep_lower=cj <= ri, keep_upper=cj >= ri,
                k_head=(one_zero(lane_k < dk), one_zero(lane_k >= dk)),
                v_head=(one_zero(lane_v < dv), one_zero(lane_v >= dv)),
                sel_hi=own & (sel_r % 2 == 0), sel_lo=own & (sel_r % 2 == 1), xsel=one_zero(xsel))


class _GlaGroup:
    def __init__(self, q_ref, k_ref, v_ref, o_ref, scans, cst, n_rows):
        self.q_ref, self.k_ref, self.v_ref, self.o_ref = q_ref, k_ref, v_ref, o_ref
        self.scans, self.cst = scans, cst
        self.dk = q_ref.shape[2] // 2
        self.dv = v_ref.shape[2] // 2
        self.nc = n_rows // CHUNK
        self.work = [(la_ref, slice(start + i * CHUNK, start + (i + 1) * CHUNK), reverse)
                     for la_ref, _, start, reverse in scans for i in range(self.nc)]
        n = len(self.work)
        self.chunks = range(n)
        self.tri2 = [cst["upper"] if rev else cst["lower"] for _, _, rev in self.work]
        self.cum, self.qe, self.ke, self.dec = [None] * n, [None] * n, [None] * n, [None] * n
        self.ke_bd, self.v_bd, self.kend_sel = [None] * n, [None] * n, [None] * n
        self.att, self.o_intra, self.m = [None] * n, [None] * n, [None] * n

    @staticmethod
    def _per_head(a, masks):
        return jnp.concatenate([a * masks[0], a * masks[1]], axis=0)

    def e1(self, w):
        la_ref, rows, _ = self.work[w]
        la = la_ref[0, rows, :]
        la_hi = la.astype(BF16)
        la_lo = (la - la_hi.astype(F32)).astype(BF16)
        self.cum[w] = _dot(self.tri2[w], jnp.concatenate([la_hi, la_lo], axis=0))

    def e2(self, w):
        _, rows, reverse = self.work[w]
        cum = self.cum[w]
        tot = cum[0:1, :] if reverse else cum[CHUNK - 1:CHUNK, :]
        self.dec[w] = jnp.exp(tot)
        self.qe[w] = (self.q_ref[0, rows, :].astype(F32) * (jnp.exp(cum) * (self.dk ** -0.5))).astype(BF16)
        self.ke[w] = self.k_ref[0, rows, :].astype(F32) * jnp.exp(-cum)

    def e3(self, w):
        cst, dec = self.cst, self.dec[w]
        self.ke_bd[w] = self._per_head(self.ke[w].astype(BF16), cst["k_head"])
        self.v_bd[w] = self._per_head(self.v_ref[0, self.work[w][1], :], cst["v_head"])
        dec_hi = dec.astype(BF16).astype(F32)
        dsel = jnp.where(cst["sel_hi"], dec_hi, jnp.where(cst["sel_lo"], dec - dec_hi, 0.0)).astype(BF16)
        self.kend_sel[w] = jnp.concatenate(
            [self._per_head((self.ke[w] * dec).astype(BF16), cst["k_head"]), dsel], axis=0)

    def m1(self, w):
        self.att[w] = lax.dot_general(self.qe[w], self.ke_bd[w], (((1,), (1,)), ((), ())),
                                      preferred_element_type=F32)

    def m2(self, w):
        keep = self.cst["keep_upper"] if self.work[w][2] else self.cst["keep_lower"]
        att = jnp.where(keep, self.att[w], 0.0).astype(BF16)
        self.o_intra[w] = _dot(att, self.v_bd[w])

    def m3(self, w):
        self.m[w] = lax.dot_general(self.kend_sel[w], jnp.concatenate([self.v_bd[w], self.cst["xsel"]], axis=0),
                                    (((0,), (0,)), ((), ())), preferred_element_type=F32)

    def finish(self, accumulate):
        dk, dv, o_ref = self.dk, self.dv, self.o_ref
        zero = jnp.zeros((dk, dv), BF16)
        for n, (_, s_ref, _, reverse) in enumerate(self.scans):
            s0, s1 = s_ref[0], s_ref[1]
            for i in (reversed(range(self.nc)) if reverse else range(self.nc)):
                w = n * self.nc + i
                rows, m = self.work[w][1], self.m[w]
                s_bd = jnp.concatenate([jnp.concatenate([s0.astype(BF16), zero], axis=1),
                                        jnp.concatenate([zero, s1.astype(BF16)], axis=1)], axis=0)
                o = self.o_intra[w] + _dot(self.qe[w], s_bd)
                o_ref[0, rows, :] = o_ref[0, rows, :] + o if accumulate else o
                s0 = s0 * m[:dk, dv:] + m[:dk, :dv]
                s1 = s1 * m[dk:, :dv] + m[dk:, dv:]
            s_ref[0] = s0
            s_ref[1] = s1


def _gla_kernel(q_ref, k_ref, v_ref, lf_ref, lb_ref, o_ref, sf_ref, sb_ref, *, group_rows):
    sf_ref[...] = jnp.zeros_like(sf_ref)
    sb_ref[...] = jnp.zeros_like(sb_ref)
    cst = _gla_constants(CHUNK, q_ref.shape[2] // 2, v_ref.shape[2] // 2)
    seq = q_ref.shape[1]
    groups = [_GlaGroup(q_ref, k_ref, v_ref, o_ref,
                        [(lf_ref, sf_ref, r0, False), (lb_ref, sb_ref, seq - group_rows - r0, True)],
                        cst, group_rows)
              for r0 in range(0, seq, group_rows)]
    first = groups[0]
    for stage in (first.e1, first.e2, first.e3):
        for w in first.chunks:
            stage(w)
    for gi, (cur, nxt) in enumerate(zip(groups, groups[1:] + [None])):
        for cur_stage, nxt_stage in ((cur.m1, "e1"), (cur.m2, "e2"), (cur.m3, "e3")):
            for w in cur.chunks:
                cur_stage(w)
                if nxt is not None:
                    getattr(nxt, nxt_stage)(w)
        cur.finish(accumulate=gi >= len(groups) // 2)


def _gla(q, k, v, laf, lab, *, group_rows):
    b, s, dk_all = q.shape
    dv_all = v.shape[2]
    pairs = GLA_HEADS // 2
    dk2, dv2 = dk_all // pairs, dv_all // pairs
    assert s % (2 * group_rows) == 0
    spec = lambda w: pl.BlockSpec((1, s, w), lambda bi, p: (bi, 0, p))
    return pl.pallas_call(
        functools.partial(_gla_kernel, group_rows=group_rows),
        grid=(b, pairs),
        in_specs=[spec(dk2), spec(dk2), spec(dv2), spec(dk2), spec(dk2)],
        out_specs=spec(dv2),
        out_shape=jax.ShapeDtypeStruct((b, s, dv_all), F32),
        scratch_shapes=[pltpu.VMEM((2, dk2 // 2, dv2 // 2), F32), pltpu.VMEM((2, dk2 // 2, dv2 // 2), F32)],
        compiler_params=pltpu.CompilerParams(dimension_semantics=("parallel", "parallel"),
                                             vmem_limit_bytes=VMEM_LIMIT),
        name="gla",
    )(q, k, v, laf, lab)


def _post_kernel(x_ref, z1r_ref, z1i_ref, og_ref, g_ref, gt_ref, onw_ref, wo_ref,
                 npost_ref, nffn_ref, wgt_ref, wup_ref, wdn_ref, nffn_post_ref, o_ref, ys, *, dv, ff_chunk):
    n2, tb, d = x_ref.shape
    groups = ys.shape[0]
    d_f = z1r_ref.shape[1]
    wo_f_ref, wo_g_ref = wo_ref.at[:d_f], wo_ref.at[d_f:]
    for kl in range(tb):
        rows = slice(kl * n2, (kl + 1) * n2)
        rhs = jnp.concatenate([z1r_ref[rows, :], z1i_ref[rows, :]], axis=0)
        y = _dot(gt_ref[kl], rhs)
        for s in range(groups):
            ys[s, pl.ds(kl, n2, stride=tb), :] = y[:, s * LANES:(s + 1) * LANES]

    halves = 2
    hn2 = n2 // halves
    hrows = hn2 * tb
    flat = lambda ref, i: ref[i * hn2:(i + 1) * hn2].reshape(hrows, ref.shape[2])
    onw = onw_ref[...]
    xs, h2s = [], []
    for i in range(halves):
        rsl = slice(i * hrows, (i + 1) * hrows)
        yf = jnp.concatenate([ys[s, rsl, :] for s in range(groups)], axis=1).astype(BF16)
        o = flat(og_ref, i)
        g = flat(g_ref, i)
        yg = []
        for hd in range(o.shape[1] // dv):
            sl = slice(hd * dv, (hd + 1) * dv)
            gh = g[:, sl]
            yg.append((_rms(o[:, sl], onw) * _silu(gh)).astype(BF16))
        m = _dot(yf, wo_f_ref[...]) + _dot(jnp.concatenate(yg, axis=1), wo_g_ref[...])
        xs.append(flat(x_ref, i) + _rms(m, npost_ref[...]))
        h2s.append(_rms(xs[i], nffn_ref[...]).astype(BF16))

    def ffn_chunk(h2, c0):
        sl = slice(c0, c0 + ff_chunk)
        gt = _dot(h2, wgt_ref[:, sl])
        up = _dot(h2, wup_ref[:, sl])
        act = (_silu(gt) * up).astype(BF16)
        return _dot(act, wdn_ref[sl, :])

    d_ff = wgt_ref.shape[1]
    starts = list(range(0, d_ff, ff_chunk))
    first = [ffn_chunk(h2s[i], starts[0]) for i in range(halves)]
    h2 = jnp.concatenate(h2s, axis=0)
    mid = None
    for c0 in starts[1:-1]:
        part = ffn_chunk(h2, c0)
        mid = part if mid is None else mid + part
    for i in range(halves):
        f = first[i] + mid[i * hrows:(i + 1) * hrows] + ffn_chunk(h2s[i], starts[-1])
        o_ref[i * hn2:(i + 1) * hn2] = (xs[i] + _rms(f, nffn_post_ref[...])).reshape(hn2, tb, d)


def _post_params(onw, w_out, npost, nffn, w_gate, w_up, w_down, nffn_post):
    vec = lambda a: a[:, None, :]
    return [vec(onw), w_out.astype(BF16), vec(npost), vec(nffn), w_gate.astype(BF16), w_up.astype(BF16),
            w_down.astype(BF16), vec(nffn_post)]


def _post(x4, z1r, z1i, og, g, gtab, layer, params, *, ff_chunk):
    batch, n1, _, d = x4.shape
    d_f = z1r.shape[3]
    d_v = og.shape[3]
    dv = d_v // GLA_HEADS
    nb = n1 // TB_POST
    z_spec = pl.BlockSpec((None, TB_POST * n1, d_f), lambda i: (i // nb, i % nb, 0))
    z3 = lambda z: z.reshape(batch, n1 * n1, d_f)
    return pl.pallas_call(
        functools.partial(_post_kernel, dv=dv, ff_chunk=ff_chunk),
        grid=(batch * nb,),
        in_specs=[_tile_spec(n1, TB_POST, d), z_spec, z_spec, _tile_spec(n1, TB_POST, d_v),
                  _tile_spec(n1, TB_POST, d_v), pl.BlockSpec((TB_POST, n1, 2 * n1), lambda i: (i % nb, 0, 0))]
                 + [_layer_spec(a, layer) for a in params],
        out_specs=_tile_spec(n1, TB_POST, d),
        out_shape=jax.ShapeDtypeStruct(x4.shape, F32),
        scratch_shapes=[pltpu.VMEM((F_GROUPS, n1 * TB_POST, LANES), F32)],
        compiler_params=pltpu.CompilerParams(dimension_semantics=("parallel",),
                                             vmem_limit_bytes=VMEM_LIMIT),
        name="post",
    )(x4, z3(z1r), z3(z1i), og, g, gtab, *params)


def kernel(x, norm_mix_pre, w_in, w_alpha_fwd, b_alpha_fwd, w_alpha_bwd, b_alpha_bwd, gla_out_norm,
           w_out, norm_mix_post, norm_ffn_pre, w_ffn_gate, w_ffn_up, w_ffn_down, norm_ffn_post):
    batch, seq, d = x.shape
    depth = w_in.shape[0]
    fc, m1, gtab = _dft_constants(seq, d // 2 // F_GROUPS)
    n1 = m1.shape[0] // 2
    x4 = x.reshape(batch, n1, n1, d)
    seq3 = lambda a: a.reshape(batch, seq, a.shape[3])
    tile4 = lambda a: a.reshape(batch, n1, n1, a.shape[2])
    in_params = _in_proj_params(norm_mix_pre, w_in, w_alpha_fwd, b_alpha_fwd, w_alpha_bwd, b_alpha_bwd)
    post_params = _post_params(gla_out_norm, w_out, norm_mix_post, norm_ffn_pre, w_ffn_gate, w_ffn_up,
                               w_ffn_down, norm_ffn_post)
    for l in range(depth):
        z1r, z1i, q, k, v, g, laf, lab = _in_proj(x4, l, in_params, fc, m1)
        og = _gla(seq3(q), seq3(k), seq3(v), seq3(laf), seq3(lab), group_rows=512)
        x4 = _post(x4, z1r, z1i, tile4(og), g, gtab, l, post_params, ff_chunk=256)
    return x4.reshape(batch, seq, d)
```

```python
import functools

import numpy as np
import jax
import jax.numpy as jnp
from jax import lax
from jax.experimental import pallas as pl
from jax.experimental.pallas import tpu as pltpu

F_GROUPS = 4
GLA_HEADS = 4
GATE_RANK = 16
GATE_LOGIT_NORMALIZER = 16.0
CHUNK = 64
EPS = 1e-6

LANES = 128
SUBLANES = 8
TB_POST = SUBLANES
TB_IN = 2 * SUBLANES
VMEM_LIMIT = 56 * 1024 * 1024

BF16 = jnp.bfloat16
F32 = jnp.float32


def _rms(x, w):
    return x * lax.rsqrt(jnp.mean(x * x, axis=-1, keepdims=True) + EPS) * w


def _silu(x):
    return (0.5 * x) * (1.0 + jnp.tanh(0.5 * x))


def _dot(a, b):
    return jnp.dot(a, b, preferred_element_type=F32)


def _const_spec(a):
    return pl.BlockSpec(a.shape, lambda *_: (0,) * a.ndim, pipeline_mode=pl.Buffered(1))


def _layer_spec(a, layer):
    return pl.BlockSpec((None,) + a.shape[1:], lambda *_: (layer,) + (0,) * (a.ndim - 1),
                        pipeline_mode=pl.Buffered(1))


def _tile_spec(n1, tb, width):
    nb = n1 // tb
    return pl.BlockSpec((None, n1, tb, width), lambda i: (i // nb, 0, i % nb, 0))


def _dft_constants(seq, group_dim):
    n1 = int(round(seq ** 0.5))
    assert n1 * n1 == seq
    c = np.arange(group_dim)
    ang = 2.0 * np.pi * ((c[:, None] * c[None, :]) % group_dim) / group_dim
    fc = np.concatenate([np.cos(ang), -np.sin(ang)], axis=1) / np.sqrt(group_dim)
    a = np.arange(n1)
    ang1 = 2.0 * np.pi * ((a[:, None] * a[None, :]) % n1) / n1
    c1, s1 = np.cos(ang1), np.sin(ang1)
    m1 = np.block([[c1, s1], [-s1, c1]]) / n1
    k1 = np.arange(n1)[:, None, None]
    k2 = np.arange(n1)[None, :, None]
    b = np.arange(n1)[None, None, :]
    ang2 = 2.0 * np.pi * ((b * (k1 + n1 * k2)) % seq) / seq
    g = np.concatenate([np.cos(ang2), np.sin(ang2)], axis=2)
    return (jnp.asarray(fc, F32).astype(BF16), jnp.asarray(m1, F32).astype(BF16),
            jnp.asarray(g, F32).astype(BF16))


def _in_proj_kernel(x_ref, nw_ref, w_ref, wgate_ref, bgate_ref, fc_ref, m1_ref,
                    z1r_ref, z1i_ref, q_ref, k_ref, v_ref, g_ref, laf_ref, lab_ref,
                    zs_r, zs_i, os_r, os_i):
    n1, tb, d = x_ref.shape
    n_sub, groups = zs_r.shape[:2]
    d_k = laf_ref.shape[2]
    d_v = v_ref.shape[2]
    d_f = z1r_ref.shape[2]
    c_qkv, c_g, c_ab = d_f, d_f + 2 * d_k + d_v, d_f + 2 * d_k + 2 * d_v
    sb = tb // n_sub
    rows = n1 * sb
    tiled = lambda val: val.reshape(n1, sb, val.shape[1])
    qkv = []
    for i in range(n_sub):
        bsl = slice(i * sb, (i + 1) * sb)
        h = _rms(x_ref[:, bsl, :].reshape(rows, d), nw_ref[...]).astype(BF16)
        p = _dot(h, w_ref[...])
        ab = p[:, c_ab:].astype(BF16)
        logits = _dot(ab, wgate_ref[...]) + bgate_ref[...]
        fp = p[:, :c_qkv].astype(BF16)
        for grp in range(groups):
            zz = _dot(fp[:, grp * LANES:(grp + 1) * LANES], fc_ref[...])
            zs_r[i, grp] = zz[:, :LANES]
            zs_i[i, grp] = zz[:, LANES:]
        qkv.append(tiled(p[:, c_qkv:c_g]))
        g_ref[:, bsl, :] = tiled(p[:, c_g:c_ab])
        la = (jnp.minimum(logits, 0.0) - jnp.log(1.0 + jnp.exp(-jnp.abs(logits)))) * (1.0 / GATE_LOGIT_NORMALIZER)
        laf_ref[:, bsl, :] = tiled(la[:, :d_k])
        lab_ref[:, bsl, :] = tiled(la[:, d_k:])
    for i in range(n_sub):
        for bl in range(sb):
            strided = pl.ds(bl, n1, stride=sb)
            zr = jnp.concatenate([zs_r[i, s, strided, :] for s in range(groups)], axis=1)
            zi = jnp.concatenate([zs_i[i, s, strided, :] for s in range(groups)], axis=1)
            out = _dot(m1_ref[...], jnp.concatenate([zr, zi], axis=0).astype(BF16))
            for s in range(groups):
                os_r[i, s, strided, :] = out[:n1, s * LANES:(s + 1) * LANES]
                os_i[i, s, strided, :] = out[n1:, s * LANES:(s + 1) * LANES]
    for s in range(groups):
        lanes = slice(s * LANES, (s + 1) * LANES)
        z1r_ref[:, :, lanes] = jnp.concatenate([tiled(os_r[i, s]) for i in range(n_sub)], axis=1).astype(BF16)
        z1i_ref[:, :, lanes] = jnp.concatenate([tiled(os_i[i, s]) for i in range(n_sub)], axis=1).astype(BF16)
    qkv = jnp.concatenate(qkv, axis=1).astype(BF16)
    q_ref[...] = qkv[:, :, :d_k]
    k_ref[...] = qkv[:, :, d_k:2 * d_k]
    v_ref[...] = qkv[:, :, 2 * d_k:2 * d_k + d_v]


def _in_proj_params(nw, w_in, w_af, b_af, w_ab, b_ab):
    d_k = w_af.shape[2]
    w = jnp.pad(w_in, ((0, 0), (0, 0), (0, LANES - 2 * GATE_RANK))).astype(BF16)
    wgate = jnp.concatenate([jnp.pad(w_af, ((0, 0), (0, 0), (0, d_k))), jnp.pad(w_ab, ((0, 0), (0, 0), (d_k, 0)))],
                            axis=1)
    wgate = jnp.pad(wgate, ((0, 0), (0, LANES - 2 * GATE_RANK), (0, 0))).astype(BF16)
    bgate = jnp.concatenate([b_af, b_ab], axis=1)[:, None, :]
    return [nw[:, None, :], w, wgate, bgate]


def _in_proj(x4, layer, params, fc, m1):
    batch, n1, _, d = x4.shape
    d_f = d // 2
    d_v = d - d_f
    d_k = d_v // 2
    assert d_f // F_GROUPS == LANES
    assert params[1].shape[2] == d_f + 2 * d_k + 2 * d_v + LANES
    consts = [fc, m1]
    outs = [(d_f, BF16), (d_f, BF16), (d_k, BF16), (d_k, BF16), (d_v, BF16), (d_v, F32), (d_k, F32), (d_k, F32)]
    n_sub = TB_IN // SUBLANES
    slab = pltpu.VMEM((n_sub, F_GROUPS, n1 * SUBLANES, LANES), F32)
    return pl.pallas_call(
        _in_proj_kernel,
        grid=(batch * (n1 // TB_IN),),
        in_specs=[_tile_spec(n1, TB_IN, d)] + [_layer_spec(a, layer) for a in params]
                 + [_const_spec(a) for a in consts],
        out_specs=[_tile_spec(n1, TB_IN, w) for w, _ in outs],
        out_shape=[jax.ShapeDtypeStruct((batch, n1, n1, w), dt) for w, dt in outs],
        scratch_shapes=[slab, slab, slab, slab],
        compiler_params=pltpu.CompilerParams(dimension_semantics=("parallel",),
                                             vmem_limit_bytes=VMEM_LIMIT),
        name="in_proj",
    )(x4, *params, *consts)


def _gla_constants(c, dk, dv):
    dk2, dv2 = 2 * dk, 2 * dv
    one_zero = lambda cond: jnp.where(cond, 1.0, 0.0).astype(BF16)
    ri = lax.broadcasted_iota(jnp.int32, (c, 2 * c), 0)
    cj = lax.broadcasted_iota(jnp.int32, (c, 2 * c), 1) % c
    lane_k = lax.broadcasted_iota(jnp.int32, (c, dk2), 1)
    lane_v = lax.broadcasted_iota(jnp.int32, (c, dv2), 1)
    sel_r = lax.broadcasted_iota(jnp.int32, (2 * SUBLANES, dk2), 0)
    sel_l = lax.broadcasted_iota(jnp.int32, (2 * SUBLANES, dk2), 1)
    xs_r = lax.broadcasted_iota(jnp.int32, (2 * SUBLANES, dv2), 0)
    xs_l = lax.broadcasted_iota(jnp.int32, (2 * SUBLANES, dv2), 1)
    own = (sel_r < 4) & ((sel_l < dk) == (sel_r < 2))
    xsel = ((xs_r < 2) & (xs_l >= dv)) | ((xs_r >= 2) & (xs_r < 4) & (xs_l < dv))
    return dict(lower=one_zero(cj <= ri), upper=one_zero(cj >= ri),
                keep_lower=cj <= ri, keep_upper=cj >= ri,
                k_head=(one_zero(lane_k < dk), one_zero(lane_k >= dk)),
                v_head=(one_zero(lane_v < dv), one_zero(lane_v >= dv)),
                sel_hi=own & (sel_r % 2 == 0), sel_lo=own & (sel_r % 2 == 1), xsel=one_zero(xsel))


class _GlaGroup:
    def __init__(self, q_ref, k_ref, v_ref, o_ref, scans, cst, n_rows):
        self.q_ref, self.k_ref, self.v_ref, self.o_ref = q_ref, k_ref, v_ref, o_ref
        self.scans, self.cst = scans, cst
        self.dk = q_ref.shape[2] // 2
        self.dv = v_ref.shape[2] // 2
        self.nc = n_rows // CHUNK
        self.work = [(la_ref, slice(start + i * CHUNK, start + (i + 1) * CHUNK), reverse)
                     for la_ref, _, start, reverse in scans for i in range(self.nc)]
        n = len(self.work)
        self.chunks = range(n)
        self.tri2 = [cst["upper"] if rev else cst["lower"] for _, _, rev in self.work]
        self.cum, self.qe, self.ke, self.dec = [None] * n, [None] * n, [None] * n, [None] * n
        self.ke_bd, self.v_bd, self.kend_sel = [None] * n, [None] * n, [None] * n
        self.att, self.att_qe, self.m = [None] * n, [None] * n, [None] * n

    @staticmethod
    def _per_head(a, masks):
        return jnp.concatenate([a * masks[0], a * masks[1]], axis=0)

    def e1(self, w):
        la_ref, rows, _ = self.work[w]
        la = la_ref[0, rows, :]
        la_hi = la.astype(BF16)
        la_lo = (la - la_hi.astype(F32)).astype(BF16)
        self.cum[w] = _dot(self.tri2[w], jnp.concatenate([la_hi, la_lo], axis=0))

    def e2(self, w):
        _, rows, reverse = self.work[w]
        cum = self.cum[w]
        tot = cum[0:1, :] if reverse else cum[CHUNK - 1:CHUNK, :]
        self.dec[w] = jnp.exp(tot)
        self.qe[w] = (self.q_ref[0, rows, :].astype(F32) * (jnp.exp(cum) * (self.dk ** -0.5))).astype(BF16)
        self.ke[w] = self.k_ref[0, rows, :].astype(F32) * jnp.exp(-cum)

    def e3(self, w):
        cst, dec = self.cst, self.dec[w]
        self.ke_bd[w] = self._per_head(self.ke[w].astype(BF16), cst["k_head"])
        self.v_bd[w] = self._per_head(self.v_ref[0, self.work[w][1], :], cst["v_head"])
        dec_hi = dec.astype(BF16).astype(F32)
        dsel = jnp.where(cst["sel_hi"], dec_hi, jnp.where(cst["sel_lo"], dec - dec_hi, 0.0)).astype(BF16)
        self.kend_sel[w] = jnp.concatenate(
            [self._per_head((self.ke[w] * dec).astype(BF16), cst["k_head"]), dsel], axis=0)

    def m1(self, w):
        self.att[w] = lax.dot_general(self.qe[w], self.ke_bd[w], (((1,), (1,)), ((), ())),
                                      preferred_element_type=F32)

    def m2(self, w):
        keep = self.cst["keep_upper"] if self.work[w][2] else self.cst["keep_lower"]
        att = jnp.where(keep, self.att[w], 0.0).astype(BF16)
        self.att_qe[w] = jnp.concatenate([att, self.qe[w]], axis=1)

    def m3(self, w):
        self.m[w] = lax.dot_general(self.kend_sel[w], jnp.concatenate([self.v_bd[w], self.cst["xsel"]], axis=0),
                                    (((0,), (0,)), ((), ())), preferred_element_type=F32)

    def finish(self, accumulate):
        dk, dv, o_ref = self.dk, self.dv, self.o_ref
        zero = jnp.zeros((dk, dv), BF16)
        for n, (_, s_ref, _, reverse) in enumerate(self.scans):
            s0, s1 = s_ref[0], s_ref[1]
            for i in (reversed(range(self.nc)) if reverse else range(self.nc)):
                w = n * self.nc + i
                rows, m = self.work[w][1], self.m[w]
                s_bd = jnp.concatenate([jnp.concatenate([s0.astype(BF16), zero], axis=1),
                                        jnp.concatenate([zero, s1.astype(BF16)], axis=1)], axis=0)
                o = _dot(self.att_qe[w], jnp.concatenate([self.v_bd[w], s_bd], axis=0))
                o_ref[0, rows, :] = o_ref[0, rows, :] + o if accumulate else o
                s0 = s0 * m[:dk, dv:] + m[:dk, :dv]
                s1 = s1 * m[dk:, :dv] + m[dk:, dv:]
            s_ref[0] = s0
            s_ref[1] = s1


def _gla_kernel(q_ref, k_ref, v_ref, lf_ref, lb_ref, o_ref, sf_ref, sb_ref, *, group_rows):
    sf_ref[...] = jnp.zeros_like(sf_ref)
    sb_ref[...] = jnp.zeros_like(sb_ref)
    cst = _gla_constants(CHUNK, q_ref.shape[2] // 2, v_ref.shape[2] // 2)
    seq = q_ref.shape[1]
    groups = [_GlaGroup(q_ref, k_ref, v_ref, o_ref,
                        [(lf_ref, sf_ref, r0, False), (lb_ref, sb_ref, seq - group_rows - r0, True)],
                        cst, group_rows)
              for r0 in range(0, seq, group_rows)]
    first = groups[0]
    for stage in (first.e1, first.e2, first.e3):
        for w in first.chunks:
            stage(w)
    for gi, (cur, nxt) in enumerate(zip(groups, groups[1:] + [None])):
        for cur_stage, nxt_stage in ((cur.m1, "e1"), (cur.m2, "e2"), (cur.m3, "e3")):
            for w in cur.chunks:
                cur_stage(w)
                if nxt is not None:
                    getattr(nxt, nxt_stage)(w)
        cur.finish(accumulate=gi >= len(groups) // 2)


def _gla(q, k, v, laf, lab, *, group_rows):
    b, s, dk_all = q.shape
    dv_all = v.shape[2]
    pairs = GLA_HEADS // 2
    dk2, dv2 = dk_all // pairs, dv_all // pairs
    assert s % (2 * group_rows) == 0
    spec = lambda w: pl.BlockSpec((1, s, w), lambda bi, p: (bi, 0, p))
    return pl.pallas_call(
        functools.partial(_gla_kernel, group_rows=group_rows),
        grid=(b, pairs),
        in_specs=[spec(dk2), spec(dk2), spec(dv2), spec(dk2), spec(dk2)],
        out_specs=spec(dv2),
        out_shape=jax.ShapeDtypeStruct((b, s, dv_all), F32),
        scratch_shapes=[pltpu.VMEM((2, dk2 // 2, dv2 // 2), F32), pltpu.VMEM((2, dk2 // 2, dv2 // 2), F32)],
        compiler_params=pltpu.CompilerParams(dimension_semantics=("parallel", "parallel"),
                                             vmem_limit_bytes=VMEM_LIMIT),
        name="gla",
    )(q, k, v, laf, lab)


def _post_kernel(x_ref, z1r_ref, z1i_ref, og_ref, g_ref, gt_ref, onw_ref, wo_ref,
                 npost_ref, nffn_ref, wgt_ref, wup_ref, wdn_ref, nffn_post_ref, o_ref, ys, *, dv, ff_chunk):
    n2, tb, d = x_ref.shape
    groups = ys.shape[0]
    d_f = z1r_ref.shape[1]
    wo_f_ref, wo_g_ref = wo_ref.at[:d_f], wo_ref.at[d_f:]
    for kl in range(tb):
        rows = slice(kl * n2, (kl + 1) * n2)
        rhs = jnp.concatenate([z1r_ref[rows, :], z1i_ref[rows, :]], axis=0)
        y = _dot(gt_ref[kl], rhs)
        for s in range(groups):
            ys[s, pl.ds(kl, n2, stride=tb), :] = y[:, s * LANES:(s + 1) * LANES]

    halves = 2
    hn2 = n2 // halves
    hrows = hn2 * tb
    flat = lambda ref, i: ref[i * hn2:(i + 1) * hn2].reshape(hrows, ref.shape[2])
    onw = onw_ref[...]
    xs, h2s = [], []
    for i in range(halves):
        rsl = slice(i * hrows, (i + 1) * hrows)
        yf = jnp.concatenate([ys[s, rsl, :] for s in range(groups)], axis=1).astype(BF16)
        o = flat(og_ref, i)
        g = flat(g_ref, i)
        yg = []
        for hd in range(o.shape[1] // dv):
            sl = slice(hd * dv, (hd + 1) * dv)
            gh = g[:, sl]
            yg.append((_rms(o[:, sl], onw) * _silu(gh)).astype(BF16))
        m = _dot(yf, wo_f_ref[...]) + _dot(jnp.concatenate(yg, axis=1), wo_g_ref[...])
        xs.append(flat(x_ref, i) + _rms(m, npost_ref[...]))
        h2s.append(_rms(xs[i], nffn_ref[...]).astype(BF16))

    def ffn_act(h2, c0):
        sl = slice(c0, c0 + ff_chunk)
        gt = _dot(h2, wgt_ref[:, sl])
        up = _dot(h2, wup_ref[:, sl])
        return (_silu(gt) * up).astype(BF16)

    d_ff = wgt_ref.shape[1]
    starts = list(range(0, d_ff, ff_chunk))
    h2 = jnp.concatenate(h2s, axis=0)
    acts = [jnp.concatenate([ffn_act(h2s[i], starts[0]) for i in range(halves)], axis=0)]
    acts += [ffn_act(h2, c0) for c0 in starts[1:]]
    act = jnp.concatenate(acts, axis=1)
    for i in range(halves):
        f = _dot(act[i * hrows:(i + 1) * hrows], wdn_ref[...])
        o_ref[i * hn2:(i + 1) * hn2] = (xs[i] + _rms(f, nffn_post_ref[...])).reshape(hn2, tb, d)


def _post_params(onw, w_out, npost, nffn, w_gate, w_up, w_down, nffn_post):
    vec = lambda a: a[:, None, :]
    return [vec(onw), w_out.astype(BF16), vec(npost), vec(nffn), w_gate.astype(BF16), w_up.astype(BF16),
            w_down.astype(BF16), vec(nffn_post)]


def _post(x4, z1r, z1i, og, g, gtab, layer, params, *, ff_chunk):
    batch, n1, _, d = x4.shape
    d_f = z1r.shape[3]
    d_v = og.shape[3]
    dv = d_v // GLA_HEADS
    nb = n1 // TB_POST
    z_spec = pl.BlockSpec((None, TB_POST * n1, d_f), lambda i: (i // nb, i % nb, 0))
    z3 = lambda z: z.reshape(batch, n1 * n1, d_f)
    return pl.pallas_call(
        functools.partial(_post_kernel, dv=dv, ff_chunk=ff_chunk),
        grid=(batch * nb,),
        in_specs=[_tile_spec(n1, TB_POST, d), z_spec, z_spec, _tile_spec(n1, TB_POST, d_v),
                  _tile_spec(n1, TB_POST, d_v), pl.BlockSpec((TB_POST, n1, 2 * n1), lambda i: (i % nb, 0, 0))]
                 + [_layer_spec(a, layer) for a in params],
        out_specs=_tile_spec(n1, TB_POST, d),
        out_shape=jax.ShapeDtypeStruct(x4.shape, F32),
        scratch_shapes=[pltpu.VMEM((F_GROUPS, n1 * TB_POST, LANES), F32)],
        compiler_params=pltpu.CompilerParams(dimension_semantics=("parallel",),
                                             vmem_limit_bytes=VMEM_LIMIT),
        name="post",
    )(x4, z3(z1r), z3(z1i), og, g, gtab, *params)


def kernel(x, norm_mix_pre, w_in, w_alpha_fwd, b_alpha_fwd, w_alpha_bwd, b_alpha_bwd, gla_out_norm,
           w_out, norm_mix_post, norm_ffn_pre, w_ffn_gate, w_ffn_up, w_ffn_down, norm_ffn_post):
    batch, seq, d = x.shape
    depth = w_in.shape[0]
    fc, m1, gtab = _dft_constants(seq, d // 2 // F_GROUPS)
    n1 = m1.shape[0] // 2
    x4 = x.reshape(batch, n1, n1, d)
    seq3 = lambda a: a.reshape(batch, seq, a.shape[3])
    tile4 = lambda a: a.reshape(batch, n1, n1, a.shape[2])
    in_params = _in_proj_params(norm_mix_pre, w_in, w_alpha_fwd, b_alpha_fwd, w_alpha_bwd, b_alpha_bwd)
    post_params = _post_params(gla_out_norm, w_out, norm_mix_post, norm_ffn_pre, w_ffn_gate, w_ffn_up,
                               w_ffn_down, norm_ffn_post)
    for l in range(depth):
        z1r, z1i, q, k, v, g, laf, lab = _in_proj(x4, l, in_params, fc, m1)
        og = _gla(seq3(q), seq3(k), seq3(v), seq3(laf), seq3(lab), group_rows=512)
        x4 = _post(x4, z1r, z1i, tile4(og), g, gtab, l, post_params, ff_chunk=256)
    return x4.reshape(batch, seq, d)
```

```python
import functools

import numpy as np
import jax
import jax.numpy as jnp
from jax import lax
from jax.experimental import pallas as pl
from jax.experimental.pallas import tpu as pltpu

F_GROUPS = 4
GLA_HEADS = 4
GATE_RANK = 16
GATE_LOGIT_NORMALIZER = 16.0
CHUNK = 64
EPS = 1e-6

LANES = 128
SUBLANES = 8
TB_POST = SUBLANES
TB_IN = 2 * SUBLANES
VMEM_LIMIT = 56 * 1024 * 1024

BF16 = jnp.bfloat16
F32 = jnp.float32


def _rms(x, w):
    return x * lax.rsqrt(jnp.mean(x * x, axis=-1, keepdims=True) + EPS) * w


def _silu(x):
    return (0.5 * x) * (1.0 + jnp.tanh(0.5 * x))


def _dot(a, b):
    return jnp.dot(a, b, preferred_element_type=F32)


def _const_spec(a):
    return pl.BlockSpec(a.shape, lambda *_: (0,) * a.ndim, pipeline_mode=pl.Buffered(1))


def _layer_spec(a, layer):
    return pl.BlockSpec((None,) + a.shape[1:], lambda *_: (layer,) + (0,) * (a.ndim - 1),
                        pipeline_mode=pl.Buffered(1))


def _tile_spec(n1, tb, width):
    nb = n1 // tb
    return pl.BlockSpec((None, n1, tb, width), lambda i: (i // nb, 0, i % nb, 0))


def _dft_constants(seq, group_dim):
    n1 = int(round(seq ** 0.5))
    assert n1 * n1 == seq
    c = np.arange(group_dim)
    ang = 2.0 * np.pi * ((c[:, None] * c[None, :]) % group_dim) / group_dim
    fc = np.concatenate([np.cos(ang), -np.sin(ang)], axis=1) / np.sqrt(group_dim)
    a = np.arange(n1)
    ang1 = 2.0 * np.pi * ((a[:, None] * a[None, :]) % n1) / n1
    c1, s1 = np.cos(ang1), np.sin(ang1)
    m1 = np.block([[c1, s1], [-s1, c1]]) / n1
    k1 = np.arange(n1)[:, None, None]
    k2 = np.arange(n1)[None, :, None]
    b = np.arange(n1)[None, None, :]
    ang2 = 2.0 * np.pi * ((b * (k1 + n1 * k2)) % seq) / seq
    g = np.concatenate([np.cos(ang2), np.sin(ang2)], axis=2)
    return (jnp.asarray(fc, F32).astype(BF16), jnp.asarray(m1, F32).astype(BF16),
            jnp.asarray(g, F32).astype(BF16))


def _in_proj_kernel(x_ref, nw_ref, w_ref, wgate_ref, bgate_ref, fc_ref, m1_ref,
                    z1r_ref, z1i_ref, q_ref, k_ref, v_ref, g_ref, laf_ref, lab_ref,
                    zs_r, zs_i, os_r, os_i):
    n1, tb, d = x_ref.shape
    n_sub, groups = zs_r.shape[:2]
    d_k = laf_ref.shape[2]
    d_v = v_ref.shape[2]
    d_f = z1r_ref.shape[2]
    c_qkv, c_g, c_ab = d_f, d_f + 2 * d_k + d_v, d_f + 2 * d_k + 2 * d_v
    sb = tb // n_sub
    rows = n1 * sb
    tiled = lambda val: val.reshape(n1, sb, val.shape[1])
    qkv = []
    for i in range(n_sub):
        bsl = slice(i * sb, (i + 1) * sb)
        h = _rms(x_ref[:, bsl, :].reshape(rows, d), nw_ref[...]).astype(BF16)
        p = lax.dot_general(h, w_ref[...], (((1,), (1,)), ((), ())), preferred_element_type=F32)
        ab = p[:, c_ab:].astype(BF16)
        logits = _dot(ab, wgate_ref[...]) + bgate_ref[...]
        fp = p[:, :c_qkv].astype(BF16)
        for grp in range(groups):
            zz = _dot(fp[:, grp * LANES:(grp + 1) * LANES], fc_ref[...])
            zs_r[i, grp] = zz[:, :LANES]
            zs_i[i, grp] = zz[:, LANES:]
        qkv.append(tiled(p[:, c_qkv:c_g]))
        g_ref[:, bsl, :] = tiled(p[:, c_g:c_ab])
        la = (jnp.minimum(logits, 0.0) - jnp.log(1.0 + jnp.exp(-jnp.abs(logits)))) * (1.0 / GATE_LOGIT_NORMALIZER)
        laf_ref[:, bsl, :] = tiled(la[:, :d_k])
        lab_ref[:, bsl, :] = tiled(la[:, d_k:])
    for i in range(n_sub):
        for bl in range(sb):
            strided = pl.ds(bl, n1, stride=sb)
            zr = jnp.concatenate([zs_r[i, s, strided, :] for s in range(groups)], axis=1)
            zi = jnp.concatenate([zs_i[i, s, strided, :] for s in range(groups)], axis=1)
            out = _dot(m1_ref[...], jnp.concatenate([zr, zi], axis=0).astype(BF16))
            for s in range(groups):
                os_r[i, s, strided, :] = out[:n1, s * LANES:(s + 1) * LANES]
                os_i[i, s, strided, :] = out[n1:, s * LANES:(s + 1) * LANES]
    for s in range(groups):
        lanes = slice(s * LANES, (s + 1) * LANES)
        z1r_ref[:, :, lanes] = jnp.concatenate([tiled(os_r[i, s]) for i in range(n_sub)], axis=1).astype(BF16)
        z1i_ref[:, :, lanes] = jnp.concatenate([tiled(os_i[i, s]) for i in range(n_sub)], axis=1).astype(BF16)
    qkv = jnp.concatenate(qkv, axis=1).astype(BF16)
    q_ref[...] = qkv[:, :, :d_k]
    k_ref[...] = qkv[:, :, d_k:2 * d_k]
    v_ref[...] = qkv[:, :, 2 * d_k:2 * d_k + d_v]


def _in_proj_params(nw, w_in, w_af, b_af, w_ab, b_ab):
    d_k = w_af.shape[2]
    w = jnp.swapaxes(w_in, 1, 2).astype(BF16)
    wgate = jnp.concatenate([jnp.pad(w_af, ((0, 0), (0, 0), (0, d_k))), jnp.pad(w_ab, ((0, 0), (0, 0), (d_k, 0)))],
                            axis=1)
    wgate = wgate.astype(BF16)
    bgate = jnp.concatenate([b_af, b_ab], axis=1)[:, None, :]
    return [nw[:, None, :], w, wgate, bgate]


def _in_proj(x4, layer, params, fc, m1):
    batch, n1, _, d = x4.shape
    d_f = d // 2
    d_v = d - d_f
    d_k = d_v // 2
    assert d_f // F_GROUPS == LANES
    consts = [fc, m1]
    outs = [(d_f, BF16), (d_f, BF16), (d_k, BF16), (d_k, BF16), (d_v, BF16), (d_v, F32), (d_k, F32), (d_k, F32)]
    n_sub = TB_IN // SUBLANES
    slab = pltpu.VMEM((n_sub, F_GROUPS, n1 * SUBLANES, LANES), F32)
    return pl.pallas_call(
        _in_proj_kernel,
        grid=(batch * (n1 // TB_IN),),
        in_specs=[_tile_spec(n1, TB_IN, d)] + [_layer_spec(a, layer) for a in params]
                 + [_const_spec(a) for a in consts],
        out_specs=[_tile_spec(n1, TB_IN, w) for w, _ in outs],
        out_shape=[jax.ShapeDtypeStruct((batch, n1, n1, w), dt) for w, dt in outs],
        scratch_shapes=[slab, slab, slab, slab],
        compiler_params=pltpu.CompilerParams(dimension_semantics=("parallel",),
                                             vmem_limit_bytes=VMEM_LIMIT),
        name="in_proj",
    )(x4, *params, *consts)


def _gla_constants(c, dk, dv):
    dk2, dv2 = 2 * dk, 2 * dv
    one_zero = lambda cond: jnp.where(cond, 1.0, 0.0).astype(BF16)
    ri = lax.broadcasted_iota(jnp.int32, (c, 2 * c), 0)
    cj = lax.broadcasted_iota(jnp.int32, (c, 2 * c), 1) % c
    lane_k = lax.broadcasted_iota(jnp.int32, (c, dk2), 1)
    lane_v = lax.broadcasted_iota(jnp.int32, (c, dv2), 1)
    sel_r = lax.broadcasted_iota(jnp.int32, (2 * SUBLANES, dk2), 0)
    sel_l = lax.broadcasted_iota(jnp.int32, (2 * SUBLANES, dk2), 1)
    xs_r = lax.broadcasted_iota(jnp.int32, (2 * SUBLANES, dv2), 0)
    xs_l = lax.broadcasted_iota(jnp.int32, (2 * SUBLANES, dv2), 1)
    own = (sel_r < 4) & ((sel_l < dk) == (sel_r < 2))
    xsel = ((xs_r < 2) & (xs_l >= dv)) | ((xs_r >= 2) & (xs_r < 4) & (xs_l < dv))
    return dict(lower=one_zero(cj <= ri), upper=one_zero(cj >= ri),
                keep_lower=cj <= ri, keep_upper=cj >= ri,
                k_head=(one_zero(lane_k < dk), one_zero(lane_k >= dk)),
                v_head=(one_zero(lane_v < dv), one_zero(lane_v >= dv)),
                sel_hi=own & (sel_r % 2 == 0), sel_lo=own & (sel_r % 2 == 1), xsel=one_zero(xsel))


class _GlaGroup:
    def __init__(self, q_ref, k_ref, v_ref, o_ref, scans, cst, n_rows):
        self.q_ref, self.k_ref, self.v_ref, self.o_ref = q_ref, k_ref, v_ref, o_ref
        self.scans, self.cst = scans, cst
        self.dk = q_ref.shape[2] // 2
        self.dv = v_ref.shape[2] // 2
        self.nc = n_rows // CHUNK
        self.work = [(la_ref, slice(start + i * CHUNK, start + (i + 1) * CHUNK), reverse)
                     for la_ref, _, start, reverse in scans for i in range(self.nc)]
        n = len(self.work)
        self.chunks = range(n)
        self.tri2 = [cst["upper"] if rev else cst["lower"] for _, _, rev in self.work]
        self.cum, self.qe, self.ke, self.dec = [None] * n, [None] * n, [None] * n, [None] * n
        self.ke_bd, self.v_bd, self.kend_sel = [None] * n, [None] * n, [None] * n
        self.att, self.att_qe, self.m = [None] * n, [None] * n, [None] * n

    @staticmethod
    def _per_head(a, masks):
        return jnp.concatenate([a * masks[0], a * masks[1]], axis=0)

    def e1(self, w):
        la_ref, rows, _ = self.work[w]
        la = la_ref[0, rows, :]
        la_hi = la.astype(BF16)
        la_lo = (la - la_hi.astype(F32)).astype(BF16)
        self.cum[w] = _dot(self.tri2[w], jnp.concatenate([la_hi, la_lo], axis=0))

    def e2(self, w):
        _, rows, reverse = self.work[w]
        cum = self.cum[w]
        tot = cum[0:1, :] if reverse else cum[CHUNK - 1:CHUNK, :]
        self.dec[w] = jnp.exp(tot)
        self.qe[w] = (self.q_ref[0, rows, :].astype(F32) * (jnp.exp(cum) * (self.dk ** -0.5))).astype(BF16)
        self.ke[w] = self.k_ref[0, rows, :].astype(F32) * jnp.exp(-cum)

    def e3(self, w):
        cst, dec = self.cst, self.dec[w]
        self.ke_bd[w] = self._per_head(self.ke[w].astype(BF16), cst["k_head"])
        self.v_bd[w] = self._per_head(self.v_ref[0, self.work[w][1], :], cst["v_head"])
        dec_hi = dec.astype(BF16).astype(F32)
        dsel = jnp.where(cst["sel_hi"], dec_hi, jnp.where(cst["sel_lo"], dec - dec_hi, 0.0)).astype(BF16)
        self.kend_sel[w] = jnp.concatenate(
            [self._per_head((self.ke[w] * dec).astype(BF16), cst["k_head"]), dsel], axis=0)

    def m1(self, w):
        self.att[w] = lax.dot_general(self.qe[w], self.ke_bd[w], (((1,), (1,)), ((), ())),
                                      preferred_element_type=F32)

    def m2(self, w):
        keep = self.cst["keep_upper"] if self.work[w][2] else self.cst["keep_lower"]
        att = jnp.where(keep, self.att[w], 0.0).astype(BF16)
        self.att_qe[w] = jnp.concatenate([att, self.qe[w]], axis=1)

    def m3(self, w):
        self.m[w] = lax.dot_general(self.kend_sel[w], jnp.concatenate([self.v_bd[w], self.cst["xsel"]], axis=0),
                                    (((0,), (0,)), ((), ())), preferred_element_type=F32)

    def finish(self, accumulate):
        dk, dv, o_ref = self.dk, self.dv, self.o_ref
        zero = jnp.zeros((dk, dv), BF16)
        for n, (_, s_ref, _, reverse) in enumerate(self.scans):
            s0, s1 = s_ref[0], s_ref[1]
            for i in (reversed(range(self.nc)) if reverse else range(self.nc)):
                w = n * self.nc + i
                rows, m = self.work[w][1], self.m[w]
                s_bd = jnp.concatenate([jnp.concatenate([s0.astype(BF16), zero], axis=1),
                                        jnp.concatenate([zero, s1.astype(BF16)], axis=1)], axis=0)
                o = _dot(self.att_qe[w], jnp.concatenate([self.v_bd[w], s_bd], axis=0))
                o_ref[0, rows, :] = o_ref[0, rows, :] + o if accumulate else o
                s0 = s0 * m[:dk, dv:] + m[:dk, :dv]
                s1 = s1 * m[dk:, :dv] + m[dk:, dv:]
            s_ref[0] = s0
            s_ref[1] = s1


def _gla_kernel(q_ref, k_ref, v_ref, lf_ref, lb_ref, o_ref, sf_ref, sb_ref, *, group_rows):
    sf_ref[...] = jnp.zeros_like(sf_ref)
    sb_ref[...] = jnp.zeros_like(sb_ref)
    cst = _gla_constants(CHUNK, q_ref.shape[2] // 2, v_ref.shape[2] // 2)
    seq = q_ref.shape[1]
    groups = [_GlaGroup(q_ref, k_ref, v_ref, o_ref,
                        [(lf_ref, sf_ref, r0, False), (lb_ref, sb_ref, seq - group_rows - r0, True)],
                        cst, group_rows)
              for r0 in range(0, seq, group_rows)]
    first = groups[0]
    for stage in (first.e1, first.e2, first.e3):
        for w in first.chunks:
            stage(w)
    for gi, (cur, nxt) in enumerate(zip(groups, groups[1:] + [None])):
        for cur_stage, nxt_stage in ((cur.m1, "e1"), (cur.m2, "e2"), (cur.m3, "e3")):
            for w in cur.chunks:
                cur_stage(w)
                if nxt is not None:
                    getattr(nxt, nxt_stage)(w)
        cur.finish(accumulate=gi >= len(groups) // 2)


def _gla(q, k, v, laf, lab, *, group_rows):
    b, s, dk_all = q.shape
    dv_all = v.shape[2]
    pairs = GLA_HEADS // 2
    dk2, dv2 = dk_all // pairs, dv_all // pairs
    assert s % (2 * group_rows) == 0
    spec = lambda w: pl.BlockSpec((1, s, w), lambda bi, p: (bi, 0, p))
    return pl.pallas_call(
        functools.partial(_gla_kernel, group_rows=group_rows),
        grid=(b, pairs),
        in_specs=[spec(dk2), spec(dk2), spec(dv2), spec(dk2), spec(dk2)],
        out_specs=spec(dv2),
        out_shape=jax.ShapeDtypeStruct((b, s, dv_all), F32),
        scratch_shapes=[pltpu.VMEM((2, dk2 // 2, dv2 // 2), F32), pltpu.VMEM((2, dk2 // 2, dv2 // 2), F32)],
        compiler_params=pltpu.CompilerParams(dimension_semantics=("parallel", "parallel"),
                                             vmem_limit_bytes=VMEM_LIMIT),
        name="gla",
    )(q, k, v, laf, lab)


def _post_kernel(x_ref, z1r_ref, z1i_ref, og_ref, g_ref, gt_ref, onw_ref, wo_ref,
                 npost_ref, nffn_ref, wgt_ref, wup_ref, wdn_ref, nffn_post_ref, o_ref, ys, *, dv, ff_chunk):
    n2, tb, d = x_ref.shape
    groups = ys.shape[0]
    d_f = z1r_ref.shape[1]
    wo_f_ref, wo_g_ref = wo_ref.at[:d_f], wo_ref.at[d_f:]
    for kl in range(tb):
        rows = slice(kl * n2, (kl + 1) * n2)
        rhs = jnp.concatenate([z1r_ref[rows, :], z1i_ref[rows, :]], axis=0)
        y = _dot(gt_ref[kl], rhs)
        for s in range(groups):
            ys[s, pl.ds(kl, n2, stride=tb), :] = y[:, s * LANES:(s + 1) * LANES]

    halves = 2
    hn2 = n2 // halves
    hrows = hn2 * tb
    flat = lambda ref, i: ref[i * hn2:(i + 1) * hn2].reshape(hrows, ref.shape[2])
    onw = onw_ref[...]
    xs, h2s = [], []
    for i in range(halves):
        rsl = slice(i * hrows, (i + 1) * hrows)
        yf = jnp.concatenate([ys[s, rsl, :] for s in range(groups)], axis=1).astype(BF16)
        o = flat(og_ref, i)
        g = flat(g_ref, i)
        yg = []
        for hd in range(o.shape[1] // dv):
            sl = slice(hd * dv, (hd + 1) * dv)
            gh = g[:, sl]
            yg.append((_rms(o[:, sl], onw) * _silu(gh)).astype(BF16))
        m = _dot(yf, wo_f_ref[...]) + _dot(jnp.concatenate(yg, axis=1), wo_g_ref[...])
        xs.append(flat(x_ref, i) + _rms(m, npost_ref[...]))
        h2s.append(_rms(xs[i], nffn_ref[...]).astype(BF16))

    def ffn_act(h2, c0):
        sl = slice(c0, c0 + ff_chunk)
        gt = _dot(h2, wgt_ref[:, sl])
        up = _dot(h2, wup_ref[:, sl])
        return (_silu(gt) * up).astype(BF16)

    d_ff = wgt_ref.shape[1]
    starts = list(range(0, d_ff, ff_chunk))
    h2 = jnp.concatenate(h2s, axis=0)
    acts = [jnp.concatenate([ffn_act(h2s[i], starts[0]) for i in range(halves)], axis=0)]
    acts += [ffn_act(h2, c0) for c0 in starts[1:]]
    act = jnp.concatenate(acts, axis=1)
    for i in range(halves):
        f = _dot(act[i * hrows:(i + 1) * hrows], wdn_ref[...])
        o_ref[i * hn2:(i + 1) * hn2] = (xs[i] + _rms(f, nffn_post_ref[...])).reshape(hn2, tb, d)


def _post_params(onw, w_out, npost, nffn, w_gate, w_up, w_down, nffn_post):
    vec = lambda a: a[:, None, :]
    return [vec(onw), w_out.astype(BF16), vec(npost), vec(nffn), w_gate.astype(BF16), w_up.astype(BF16),
            w_down.astype(BF16), vec(nffn_post)]


def _post(x4, z1r, z1i, og, g, gtab, layer, params, *, ff_chunk):
    batch, n1, _, d = x4.shape
    d_f = z1r.shape[3]
    d_v = og.shape[3]
    dv = d_v // GLA_HEADS
    nb = n1 // TB_POST
    z_spec = pl.BlockSpec((None, TB_POST * n1, d_f), lambda i: (i // nb, i % nb, 0))
    z3 = lambda z: z.reshape(batch, n1 * n1, d_f)
    return pl.pallas_call(
        functools.partial(_post_kernel, dv=dv, ff_chunk=ff_chunk),
        grid=(batch * nb,),
        in_specs=[_tile_spec(n1, TB_POST, d), z_spec, z_spec, _tile_spec(n1, TB_POST, d_v),
                  _tile_spec(n1, TB_POST, d_v), pl.BlockSpec((TB_POST, n1, 2 * n1), lambda i: (i % nb, 0, 0))]
                 + [_layer_spec(a, layer) for a in params],
        out_specs=_tile_spec(n1, TB_POST, d),
        out_shape=jax.ShapeDtypeStruct(x4.shape, F32),
        scratch_shapes=[pltpu.VMEM((F_GROUPS, n1 * TB_POST, LANES), F32)],
        compiler_params=pltpu.CompilerParams(dimension_semantics=("parallel",),
                                             vmem_limit_bytes=VMEM_LIMIT),
        name="post",
    )(x4, z3(z1r), z3(z1i), og, g, gtab, *params)


def kernel(x, norm_mix_pre, w_in, w_alpha_fwd, b_alpha_fwd, w_alpha_bwd, b_alpha_bwd, gla_out_norm,
           w_out, norm_mix_post, norm_ffn_pre, w_ffn_gate, w_ffn_up, w_ffn_down, norm_ffn_post):
    batch, seq, d = x.shape
    depth = w_in.shape[0]
    fc, m1, gtab = _dft_constants(seq, d // 2 // F_GROUPS)
    n1 = m1.shape[0] // 2
    x4 = x.reshape(batch, n1, n1, d)
    seq3 = lambda a: a.reshape(batch, seq, a.shape[3])
    tile4 = lambda a: a.reshape(batch, n1, n1, a.shape[2])
    in_params = _in_proj_params(norm_mix_pre, w_in, w_alpha_fwd, b_alpha_fwd, w_alpha_bwd, b_alpha_bwd)
    post_params = _post_params(gla_out_norm, w_out, norm_mix_post, norm_ffn_pre, w_ffn_gate, w_ffn_up,
                               w_ffn_down, norm_ffn_post)
    for l in range(depth):
        z1r, z1i, q, k, v, g, laf, lab = _in_proj(x4, l, in_params, fc, m1)
        og = _gla(seq3(q), seq3(k), seq3(v), seq3(laf), seq3(lab), group_rows=512)
        x4 = _post(x4, z1r, z1i, tile4(og), g, gtab, l, post_params, ff_chunk=256)
    return x4.reshape(batch, seq, d)
```
